```python
import math
import jax, jax.numpy as jnp
from jax import lax
import numpy as np

D_MODEL = 1024
BATCH = 8
SEQ = 2048
DEPTH = 1
DEC_BATCH = 128
DEC_SEQ = 4
PAST_LEN = 16384
PAGE_SIZE = 128

CONV_DIM = D_MODEL
CONV_WIDTH = 31
SSM_EXPAND = 2
SSM_DIM = SSM_EXPAND * D_MODEL
SSM_HEAD_DIM = 64
SSM_HEADS = SSM_DIM // SSM_HEAD_DIM
SSM_GROUPS = 8
SSM_STATE = 128
SSM_CONV_WIDTH = 4
SSM_CHUNK = 128
SSM_XBC = SSM_DIM + 2 * SSM_GROUPS * SSM_STATE
IN_SIZES = (CONV_DIM, CONV_DIM, CONV_DIM, SSM_DIM, SSM_XBC, SSM_HEADS, D_MODEL, D_MODEL)
IN_PROJ_DIM = 3 * CONV_DIM + SSM_DIM + SSM_XBC + SSM_HEADS + 2 * D_MODEL
EPS = 1e-6

kernel_name = 'hybrid_conformer_ssd_decode_step'


def split_cols(t, sizes):
    offs = []
    acc = 0
    for s in sizes[:-1]:
        acc += s
        offs.append(acc)
    return jnp.split(t, offs, axis=-1)


def rmsnorm(x, g):
    x32 = x.astype(jnp.float32)
    r = lax.rsqrt(jnp.mean(x32 * x32, axis=-1, keepdims=True) + EPS)
    return (x32 * r).astype(x.dtype) * g


def layernorm(x, g, b):
    x32 = x.astype(jnp.float32)
    mu = jnp.mean(x32, axis=-1, keepdims=True)
    var = jnp.mean(jnp.square(x32 - mu), axis=-1, keepdims=True)
    return ((x32 - mu) * lax.rsqrt(var + EPS)).astype(x.dtype) * g + b


def causal_dwconv(x, buf, w, b):
    width, ch = w.shape
    xp = jnp.concatenate([buf.astype(x.dtype), x], axis=1)
    y = lax.conv_general_dilated(xp, w[:, None, :].astype(x.dtype), window_strides=(1,),
                                 padding='VALID', dimension_numbers=('NWC', 'WIO', 'NWC'),
                                 feature_group_count=ch)
    return y + b, xp[:, -(width - 1):]


def ssd_scan(x, dt, a, bm, cm, h0, chunk):
    bsz, l, nh, p = x.shape
    g, n = bm.shape[2], bm.shape[3]
    r = nh // g
    q = chunk
    c = l // q
    f32 = jnp.float32
    xc = x.reshape(bsz, c, q, g, r, p).astype(f32)
    dtc = dt.reshape(bsz, c, q, g, r).astype(f32)
    bc = bm.reshape(bsz, c, q, g, n).astype(f32)
    cc = cm.reshape(bsz, c, q, g, n).astype(f32)
    cum = jnp.cumsum(dtc * a.reshape(g, r).astype(f32), axis=2)
    xdt = xc * dtc[..., None]
    seg = cum[:, :, :, None] - cum[:, :, None]
    mask = jnp.tril(jnp.ones((q, q), dtype=bool))[:, :, None, None]
    decay = jnp.exp(jnp.where(mask, seg, -jnp.inf))
    cb = jnp.einsum('bcqgn,bckgn->bcqkg', cc, bc)
    y_intra = jnp.einsum('bcqkgr,bckgrp->bcqgrp', cb[..., None] * decay, xdt)
    decay_end = jnp.exp(cum[:, :, -1:] - cum)
    st = jnp.einsum('bcqgn,bcqgrp->bcgrpn', bc, xdt * decay_end[..., None])
    chunk_decay = jnp.exp(cum[:, :, -1])

    def step(h_prev, inp):
        s, d = inp
        return d[..., None, None] * h_prev + s, h_prev

    h_last, h_starts = lax.scan(step, h0.astype(f32).reshape(bsz, g, r, p, n),
                                (jnp.moveaxis(st, 1, 0), jnp.moveaxis(chunk_decay, 1, 0)))
    h_starts = jnp.moveaxis(h_starts, 0, 1)
    y_inter = jnp.einsum('bcqgn,bcgrpn->bcqgrp', cc, h_starts) * jnp.exp(cum)[..., None]
    y = (y_intra + y_inter).reshape(bsz, l, nh, p).astype(x.dtype)
    return y, h_last.reshape(bsz, nh, p, n).astype(x.dtype)


def layer(x, c, conf_buf, ssm_buf, ssm_h, w_ada, b_ada, g_pre, g_post, w_in,
          conf_dw_w, conf_dw_b, conf_ln_g, conf_ln_b, w_conf_out,
          ssm_dw_w, ssm_dw_b, ssm_dt_bias, ssm_a_log, ssm_d, ssm_norm_g, w_ssm_out, w_o):
    bsz, L, _ = x.shape
    shift, scale, gate = jnp.split(jax.nn.silu(c) @ w_ada + b_ada, 3, axis=-1)
    h = rmsnorm(x, g_pre) * (1.0 + scale[:, None]) + shift[:, None]
    proj = h @ w_in
    conv_val, conv_glu, conv_gate, z, xbc, dt_raw, m_conv, m_ssm = split_cols(proj, IN_SIZES)
    u = conv_val * jax.nn.sigmoid(conv_glu)
    u, new_conf_buf = causal_dwconv(u, conf_buf, conf_dw_w, conf_dw_b)
    u = jax.nn.silu(layernorm(u, conf_ln_g, conf_ln_b)) * jax.nn.silu(conv_gate)
    branch_conv = u @ w_conf_out
    xbc, new_ssm_buf = causal_dwconv(xbc, ssm_buf, ssm_dw_w, ssm_dw_b)
    xbc = jax.nn.silu(xbc)
    xs, bm, cm = split_cols(xbc, (SSM_DIM, SSM_GROUPS * SSM_STATE, SSM_GROUPS * SSM_STATE))
    xs = xs.reshape(bsz, L, SSM_HEADS, SSM_HEAD_DIM)
    bm = bm.reshape(bsz, L, SSM_GROUPS, SSM_STATE)
    cm = cm.reshape(bsz, L, SSM_GROUPS, SSM_STATE)
    dt = jax.nn.softplus(dt_raw + ssm_dt_bias)
    a = -jnp.exp(ssm_a_log.astype(jnp.float32))
    chunk = SSM_CHUNK if L % SSM_CHUNK == 0 else L
    y, new_h = ssd_scan(xs, dt, a, bm, cm, ssm_h, chunk)
    y = y + ssm_d[:, None] * xs
    y = y.reshape(bsz, L, SSM_DIM) * jax.nn.silu(z)
    y = rmsnorm(y.reshape(bsz, L, SSM_GROUPS, SSM_DIM // SSM_GROUPS), 1.0).reshape(bsz, L, SSM_DIM) * ssm_norm_g
    branch_ssm = y @ w_ssm_out
    merged = jax.nn.sigmoid(m_conv) * branch_conv + jax.nn.sigmoid(m_ssm) * branch_ssm
    o = merged @ w_o
    return x + gate[:, None] * rmsnorm(o, g_post), new_conf_buf, new_ssm_buf, new_h


def setup_inputs(seed: int = 0) -> dict:
    key = jax.random.key(seed)
    ks = jax.random.split(key, 26)
    f32 = jnp.float32

    def nrm(k, shape, s):
        return jax.random.normal(k, shape, f32) * s

    dt0 = jnp.exp(jax.random.uniform(ks[19], (DEPTH, SSM_HEADS), f32,
                                     minval=math.log(1e-3), maxval=math.log(1e-1)))
    return {
        'x_prompt': nrm(ks[0], (BATCH, SEQ, D_MODEL), 1.0),
        'x_sample': nrm(ks[1], (DEC_BATCH, DEC_SEQ, D_MODEL), 1.0),
        'c_prompt': nrm(ks[2], (BATCH, D_MODEL), 1.0),
        'c_sample': nrm(ks[3], (DEC_BATCH, D_MODEL), 1.0),
        'state_conf_conv': nrm(ks[4], (DEPTH, DEC_BATCH, CONV_WIDTH - 1, CONV_DIM), 1.0),
        'state_ssm_conv': nrm(ks[5], (DEPTH, DEC_BATCH, SSM_CONV_WIDTH - 1, SSM_XBC), 1.0),
        'state_ssm': nrm(ks[6], (DEPTH, DEC_BATCH, SSM_HEADS, SSM_HEAD_DIM, SSM_STATE), 0.1),
        'w_ada': nrm(ks[7], (DEPTH, D_MODEL, 3 * D_MODEL), 0.1 * D_MODEL ** -0.5),
        'b_ada': nrm(ks[8], (DEPTH, 3 * D_MODEL), 0.01),
        'g_pre': 1.0 + nrm(ks[9], (DEPTH, D_MODEL), 0.01),
        'g_post': 1.0 + nrm(ks[10], (DEPTH, D_MODEL), 0.01),
        'w_in': nrm(ks[11], (DEPTH, D_MODEL, IN_PROJ_DIM), D_MODEL ** -0.5),
        'conf_dw_w': nrm(ks[12], (DEPTH, CONV_WIDTH, CONV_DIM), CONV_WIDTH ** -0.5),
        'conf_dw_b': nrm(ks[13], (DEPTH, CONV_DIM), 0.01),
        'conf_ln_g': 1.0 + nrm(ks[14], (DEPTH, CONV_DIM), 0.01),
        'conf_ln_b': nrm(ks[15], (DEPTH, CONV_DIM), 0.01),
        'w_conf_out': nrm(ks[16], (DEPTH, CONV_DIM, D_MODEL), CONV_DIM ** -0.5),
        'ssm_dw_w': nrm(ks[17], (DEPTH, SSM_CONV_WIDTH, SSM_XBC), SSM_CONV_WIDTH ** -0.5),
        'ssm_dw_b': nrm(ks[18], (DEPTH, SSM_XBC), 0.01),
        'ssm_dt_bias': dt0 + jnp.log(-jnp.expm1(-dt0)),
        'ssm_a_log': jnp.log(jax.random.uniform(ks[20], (DEPTH, SSM_HEADS), f32, minval=1.0, maxval=16.0)),
        'ssm_d': 1.0 + nrm(ks[21], (DEPTH, SSM_HEADS), 0.01),
        'ssm_norm_g': 1.0 + nrm(ks[22], (DEPTH, SSM_DIM), 0.01),
        'w_ssm_out': nrm(ks[23], (DEPTH, SSM_DIM, D_MODEL), SSM_DIM ** -0.5),
        'w_o': nrm(ks[24], (DEPTH, D_MODEL, D_MODEL), D_MODEL ** -0.5),
    }


def reference(x_prompt, x_sample, c_prompt, c_sample, state_conf_conv, state_ssm_conv, state_ssm,
              w_ada, b_ada, g_pre, g_post, w_in, conf_dw_w, conf_dw_b, conf_ln_g, conf_ln_b, w_conf_out,
              ssm_dw_w, ssm_dw_b, ssm_dt_bias, ssm_a_log, ssm_d, ssm_norm_g, w_ssm_out, w_o):
    xp = x_prompt
    xs = x_sample
    bp = x_prompt.shape[0]
    pc_conf, pc_ssmc, pc_ssm = [], [], []
    sc_conf, sc_ssmc, sc_ssm = [], [], []
    for l in range(DEPTH):
        wl = (w_ada[l], b_ada[l], g_pre[l], g_post[l], w_in[l], conf_dw_w[l], conf_dw_b[l],
              conf_ln_g[l], conf_ln_b[l], w_conf_out[l], ssm_dw_w[l], ssm_dw_b[l], ssm_dt_bias[l],
              ssm_a_log[l], ssm_d[l], ssm_norm_g[l], w_ssm_out[l], w_o[l])
        zero_conf = jnp.zeros((bp, CONV_WIDTH - 1, CONV_DIM), xp.dtype)
        zero_ssmc = jnp.zeros((bp, SSM_CONV_WIDTH - 1, SSM_XBC), xp.dtype)
        zero_h = jnp.zeros((bp, SSM_HEADS, SSM_HEAD_DIM, SSM_STATE), xp.dtype)
        xp, a1, a2, a3 = layer(xp, c_prompt, zero_conf, zero_ssmc, zero_h, *wl)
        xs, b1, b2, b3 = layer(xs, c_sample, state_conf_conv[l], state_ssm_conv[l], state_ssm[l], *wl)
        pc_conf.append(a1)
        pc_ssmc.append(a2)
        pc_ssm.append(a3)
        sc_conf.append(b1)
        sc_ssmc.append(b2)
        sc_ssm.append(b3)
    return (xp, xs, jnp.stack(pc_conf), jnp.stack(pc_ssmc), jnp.stack(pc_ssm),
            jnp.stack(sc_conf), jnp.stack(sc_ssmc), jnp.stack(sc_ssm))
```

```python
import functools

import jax
import jax.numpy as jnp
import numpy as np
from jax import lax
from jax.experimental import pallas as pl
from jax.experimental.pallas import tpu as pltpu

F32 = jnp.float32
BF16 = jnp.bfloat16

D_MODEL = 1024
CONV_WIDTH = 31
SSM_DIM = 2048
SSM_HEADS = 32
SSM_HEAD_DIM = 64
SSM_GROUPS = 8
SSM_STATE = 128
SSM_CONV_WIDTH = 4
SSM_BC = SSM_GROUPS * SSM_STATE
SSM_XBC = SSM_DIM + 2 * SSM_BC
SSM_CHUNK = 128
GROUP_DIM = SSM_DIM // SSM_GROUPS
HEADS_PER_GROUP = SSM_HEADS // SSM_GROUPS
EPS = 1e-6

SUBLANES = 8
LANES = 128
VMEM_LIMIT = 60 * 1024 * 1024

PROMPT_TILE = 256
CONF_HALO = 32
SSM_HALO = 8

SAMPLE_SEQ_BLOCK = 32
STATE_SEQ_BLOCK = 8
POST_ROWS = 256


def _sig(x):
    return 0.5 * jnp.tanh(0.5 * x) + 0.5


def _silu(x):
    return x * _sig(x)


def _dot(a, b):
    return jnp.dot(a, b, preferred_element_type=F32)


def _dot_nt(a, b):
    return lax.dot_general(a, b, (((1,), (1,)), ((), ())), preferred_element_type=F32)


def _dot_tn(a, b):
    return lax.dot_general(a, b, (((0,), (0,)), ((), ())), preferred_element_type=F32)


def _split3(v):
    hi = v.astype(BF16)
    r1 = v - hi.astype(F32)
    mid = r1.astype(BF16)
    lo = (r1 - mid.astype(F32)).astype(BF16)
    return hi, mid, lo


def _exact_left(sel3, v):
    hi, mid, lo = _split3(v)
    return _dot(sel3, jnp.concatenate([hi, mid, lo], axis=0))


def _exact_right(v, sel3):
    hi, mid, lo = _split3(v)
    return _dot(jnp.concatenate([hi, mid, lo], axis=1), sel3)


def _softplus(x):
    return jnp.maximum(x, 0.0) + jnp.log1p(jnp.exp(-jnp.abs(x)))


def _prenorm(x, g_pre, scale, shift):
    r = lax.rsqrt(jnp.mean(x * x, axis=-1, keepdims=True) + EPS)
    return ((x * r) * g_pre) * (1.0 + scale) + shift


def _layernorm(x, g, b):
    mu = jnp.mean(x, axis=-1, keepdims=True)
    xc = x - mu
    var = jnp.mean(xc * xc, axis=-1, keepdims=True)
    return (xc * lax.rsqrt(var + EPS)) * g + b


def _group_rmsnorm(y, g):
    outs = []
    for gi in range(SSM_GROUPS):
        yg = y[:, gi * GROUP_DIM:(gi + 1) * GROUP_DIM]
        r = lax.rsqrt(jnp.mean(yg * yg, axis=-1, keepdims=True) + EPS)
        outs.append((yg * r) * g[:, gi * GROUP_DIM:(gi + 1) * GROUP_DIM])
    return jnp.concatenate(outs, axis=1)


def _final(x, gate, o, g_post):
    r = lax.rsqrt(jnp.mean(o * o, axis=-1, keepdims=True) + EPS)
    return x + gate * ((o * r) * g_post)


def _ssd_intra_group(cg, bg, cum, cum_t, mask, xdt_g, g):
    rows = cg.shape[0]
    cb = _dot_nt(cg, bg)
    lane = lax.broadcasted_iota(jnp.int32, (rows, GROUP_DIM), 1)
    ms, xbd = [], []
    for r in range(HEADS_PER_GROUP):
        hd = g * HEADS_PER_GROUP + r
        seg = cum[:, hd:hd + 1] - cum_t[hd:hd + 1, :]
        dec = jnp.exp(jnp.where(mask, seg, -jnp.inf))
        ms.append((cb * dec).astype(BF16))
        in_head = (lane >= r * SSM_HEAD_DIM) & (lane < (r + 1) * SSM_HEAD_DIM)
        xbd.append(jnp.where(in_head, xdt_g, jnp.zeros_like(xdt_g)))
    return _dot(jnp.concatenate(ms, axis=1), jnp.concatenate(xbd, axis=0))


def _ada_kernel(c_ref, w_ref, b_ref, o_ref):
    s = _silu(c_ref[...]).astype(BF16)
    o_ref[...] = _dot(s, w_ref[...].astype(BF16)) + b_ref[...]


def _ada(c_all, w_ada, b_ada):
    n = c_all.shape[0]
    return pl.pallas_call(
        _ada_kernel,
        grid=(3,),
        in_specs=[
            pl.BlockSpec((n, D_MODEL), lambda k: (0, 0)),
            pl.BlockSpec((D_MODEL, D_MODEL), lambda k: (0, k)),
            pl.BlockSpec((1, D_MODEL), lambda k: (0, k)),
        ],
        out_specs=pl.BlockSpec((n, D_MODEL), lambda k: (0, k)),
        out_shape=jax.ShapeDtypeStruct((n, 3 * D_MODEL), F32),
        name="ada_mod",
    )(c_all, w_ada, b_ada)


def _prompt_kernel(x_ref, sh_ref, sc_ref, gt_ref, gpre_ref, gpost_ref, cw_ref, cb_ref, lng_ref, lnb_ref,
                   sw_ref, sb_ref, dtb_ref, alog_ref, dexp_ref, ng_ref, tri_ref, e3_ref,
                   wv_ref, wz_ref, wx_ref, wdt_ref, wm_ref, wco_ref, wso_ref, wo_ref,
                   y_out, conf_out, ssmc_out, st_out,
                   h_ref, ubuf, xbuf, ht_ref, yb_ref):
    T = PROMPT_TILE
    j = pl.program_id(1)
    last = pl.num_programs(1) - 1

    @pl.when(j == 0)
    def _():
        ubuf[0:CONF_HALO, :] = jnp.zeros((CONF_HALO, D_MODEL), F32)
        xbuf[0:SSM_HALO, :] = jnp.zeros((SSM_HALO, SSM_XBC), F32)
        ht_ref[...] = jnp.zeros_like(ht_ref)

    h_ref[...] = _prenorm(x_ref[...], gpre_ref[...], sc_ref[...], sh_ref[...]).astype(BF16)

    u = _dot(h_ref[...], wv_ref[:, 0:D_MODEL]) * _sig(_dot(h_ref[...], wv_ref[:, D_MODEL:2 * D_MODEL]))
    ubuf[CONF_HALO:CONF_HALO + T, :] = u
    base = CONF_HALO - (CONV_WIDTH - 1)
    acc = jnp.broadcast_to(cb_ref[...], (T, D_MODEL))
    for p in range(SUBLANES):
        rows = T if p == 0 else T + SUBLANES
        z = None
        for k in range(CONV_WIDTH):
            if (base + k) % SUBLANES != p:
                continue
            off = base + k - p
            term = cw_ref[k:k + 1, :] * ubuf[off:off + rows, :]
            z = term if z is None else z + term
        acc = acc + (z if p == 0 else z[p:p + T, :])
    uc = _silu(_layernorm(acc, lng_ref[...], lnb_ref[...])) * _silu(
        _dot(h_ref[...], wv_ref[:, 2 * D_MODEL:3 * D_MODEL]))
    branch_conv = _dot(uc.astype(BF16), wco_ref[...])

    @pl.when(j == last)
    def _():
        conf_out[...] = ubuf[T + CONF_HALO - (CONV_WIDTH - 1):T + CONF_HALO, :]

    ubuf[0:CONF_HALO, :] = ubuf[T:T + CONF_HALO, :]

    xbuf[SSM_HALO:SSM_HALO + T, :] = _dot(h_ref[...], wx_ref[...])
    sbase = SSM_HALO - (SSM_CONV_WIDTH - 1)
    xacc = jnp.broadcast_to(sb_ref[...], (T, SSM_XBC))
    for k in range(SSM_CONV_WIDTH):
        xacc = xacc + sw_ref[k:k + 1, :] * xbuf[sbase + k:sbase + k + T, :]
    xbc = _silu(xacc)

    @pl.when(j == last)
    def _():
        ssmc_out[...] = xbuf[T + SSM_HALO - (SSM_CONV_WIDTH - 1):T + SSM_HALO, :]

    xbuf[0:SSM_HALO, :] = xbuf[T:T + SSM_HALO, :]

    xs = xbc[:, 0:SSM_DIM]
    bm = xbc[:, SSM_DIM:SSM_DIM + SSM_BC].astype(BF16)
    cm = xbc[:, SSM_DIM + SSM_BC:].astype(BF16)

    dt = _softplus(_dot(h_ref[...], wdt_ref[...]) + dtb_ref[...])
    dta = dt * (-jnp.exp(alog_ref[...]))

    row = lax.broadcasted_iota(jnp.int32, (SSM_CHUNK, SSM_CHUNK), 0)
    col = lax.broadcasted_iota(jnp.int32, (SSM_CHUNK, SSM_CHUNK), 1)
    causal = col <= row

    for c in range(T // SSM_CHUNK):
        rs = slice(c * SSM_CHUNK, (c + 1) * SSM_CHUNK)
        cum = _exact_left(tri_ref[...], dta[rs, :])
        cum_last = cum[SSM_CHUNK - 1:SSM_CHUNK, :]
        cum_t = cum.T
        fac = jnp.concatenate([
            dt[rs, :],
            jnp.exp(cum_last - cum),
            jnp.exp(cum),
            jnp.broadcast_to(jnp.exp(cum_last), (SUBLANES, LANES)),
        ], axis=0)
        facx = _exact_right(fac, e3_ref[...])
        dt_x = facx[0:SSM_CHUNK]
        dend_x = facx[SSM_CHUNK:2 * SSM_CHUNK]
        ecum_x = facx[2 * SSM_CHUNK:3 * SSM_CHUNK]
        cdec_x = facx[3 * SSM_CHUNK:3 * SSM_CHUNK + 1]
        xdt = xs[rs, :] * dt_x
        xdt_b = xdt.astype(BF16)
        xdte_b = (xdt * dend_x).astype(BF16)
        for g in range(SSM_GROUPS):
            gs = slice(g * GROUP_DIM, (g + 1) * GROUP_DIM)
            ns = slice(g * SSM_STATE, (g + 1) * SSM_STATE)
            bg = bm[rs, ns]
            cg = cm[rs, ns]
            htg = ht_ref[:, gs]
            y_g = _dot(cg, htg.astype(BF16)) * ecum_x[:, gs]
            y_g = y_g + _ssd_intra_group(cg, bg, cum, cum_t, causal, xdt_b[:, gs], g)
            yb_ref[rs, gs] = y_g
            ht_ref[:, gs] = cdec_x[:, gs] * htg + _dot_tn(bg, xdte_b[:, gs])

    @pl.when(j == last)
    def _():
        st_out[...] = ht_ref[...].T

    y = (yb_ref[...] + dexp_ref[...] * xs) * _silu(_dot(h_ref[...], wz_ref[...]))
    branch_ssm = _dot(_group_rmsnorm(y, ng_ref[...]).astype(BF16), wso_ref[...])

    m = _dot(h_ref[...], wm_ref[...])
    merged = _sig(m[:, 0:D_MODEL]) * branch_conv + _sig(m[:, D_MODEL:]) * branch_ssm
    o = _dot(merged.astype(BF16), wo_ref[...])
    y_out[...] = _final(x_ref[...], gt_ref[...], o, gpost_ref[...])


def _const(shape, ngrid):
    nd = len(shape)
    return pl.BlockSpec(shape, lambda *_: (0,) * nd, pipeline_mode=pl.Buffered(1))


def _prompt_call(x, mod, p):
    bsz, seq, _ = x.shape
    T = PROMPT_TILE
    nt = seq // T
    mod3 = mod.reshape(bsz, 1, 3 * D_MODEL)
    consts = (p["g_pre"], p["g_post"], p["cw"], p["cb"], p["lng"], p["lnb"], p["sw"], p["sb"], p["dtb"],
              p["alog"], p["dexp"], p["ng"], p["tri3"], p["e3"],
              p["w_v"], p["w_z"], p["w_x"], p["w_dt"], p["w_m"], p["w_co"], p["w_so"], p["w_o"])
    mod_specs = [pl.BlockSpec((None, 1, D_MODEL), functools.partial(lambda k, b, j: (b, 0, k), k))
                 for k in range(3)]
    in_specs = ([pl.BlockSpec((None, T, D_MODEL), lambda b, j: (b, j, 0))] + mod_specs
                + [_const(c.shape, 2) for c in consts])
    out_shape = (
        jax.ShapeDtypeStruct((bsz, seq, D_MODEL), F32),
        jax.ShapeDtypeStruct((bsz, CONV_WIDTH - 1, D_MODEL), F32),
        jax.ShapeDtypeStruct((bsz, SSM_CONV_WIDTH - 1, SSM_XBC), F32),
        jax.ShapeDtypeStruct((bsz, SSM_DIM, SSM_STATE), F32),
    )
    out_specs = (
        pl.BlockSpec((None, T, D_MODEL), lambda b, j: (b, j, 0)),
        pl.BlockSpec((None, CONV_WIDTH - 1, D_MODEL), lambda b, j: (b, 0, 0)),
        pl.BlockSpec((None, SSM_CONV_WIDTH - 1, SSM_XBC), lambda b, j: (b, 0, 0)),
        pl.BlockSpec((None, SSM_DIM, SSM_STATE), lambda b, j: (b, 0, 0)),
    )
    scratch = [
        pltpu.VMEM((T, D_MODEL), BF16),
        pltpu.VMEM((T + CONF_HALO, D_MODEL), F32),
        pltpu.VMEM((T + SSM_HALO, SSM_XBC), F32),
        pltpu.VMEM((SSM_STATE, SSM_DIM), F32),
        pltpu.VMEM((T, SSM_DIM), F32),
    ]
    return pl.pallas_call(
        _prompt_kernel,
        grid=(bsz, nt),
        in_specs=in_specs,
        out_specs=out_specs,
        out_shape=out_shape,
        scratch_shapes=scratch,
        compiler_params=pltpu.CompilerParams(
            dimension_semantics=("arbitrary", "arbitrary"), vmem_limit_bytes=VMEM_LIMIT),
        name="prompt_layer",
    )(x, mod3, mod3, mod3, *consts)


def _sample_pre_kernel(x_ref, sh_ref, sc_ref, cst_ref, sst_ref,
                       gpre_ref, cw_ref, cb_ref, lng_ref, lnb_ref, sw_ref, sb_ref, dtb_ref, alog_ref,
                       dexp_ref, e3_ref, wv_ref, wx_ref, wdt_ref, wco_ref,
                       h_out, bc_out, unew_out, xnew_out, yp_out, ec_out, c_out, b_out, xd_out, dec_out):
    nt, sb = x_ref.shape[0], x_ref.shape[1]
    rows = nt * sb

    def tile_rows(v):
        return jnp.concatenate([v] * nt, axis=0)

    x = x_ref[...].reshape(rows, D_MODEL)
    hb = _prenorm(x, gpre_ref[...], tile_rows(sc_ref[...]), tile_rows(sh_ref[...])).astype(BF16)
    h_out[...] = hb.reshape(nt, sb, D_MODEL)

    u = _dot(hb, wv_ref[:, 0:D_MODEL]) * _sig(_dot(hb, wv_ref[:, D_MODEL:2 * D_MODEL]))
    unew_out[...] = u.reshape(nt, sb, D_MODEL)

    def conf_ext(jj):
        if jj < CONV_WIDTH - 1:
            return cst_ref[jj]
        return u[(jj - (CONV_WIDTH - 1)) * sb:(jj - (CONV_WIDTH - 2)) * sb, :]

    accs = []
    for t in range(nt):
        acc = jnp.broadcast_to(cb_ref[...], (sb, D_MODEL))
        for k in range(CONV_WIDTH):
            acc = acc + cw_ref[k:k + 1, :] * conf_ext(t + k)
        accs.append(acc)
    acc = jnp.concatenate(accs, axis=0)
    uc = _silu(_layernorm(acc, lng_ref[...], lnb_ref[...])) * _silu(_dot(hb, wv_ref[:, 2 * D_MODEL:3 * D_MODEL]))
    bc_out[...] = _dot(uc.astype(BF16), wco_ref[...]).reshape(nt, sb, D_MODEL)

    xpre = _dot(hb, wx_ref[...])
    xnew_out[...] = xpre[(nt - (SSM_CONV_WIDTH - 1)) * sb:, :].reshape(SSM_CONV_WIDTH - 1, sb, SSM_XBC)

    def ssm_ext(jj):
        if jj < SSM_CONV_WIDTH - 1:
            return sst_ref[jj]
        return xpre[(jj - (SSM_CONV_WIDTH - 1)) * sb:(jj - (SSM_CONV_WIDTH - 2)) * sb, :]

    xaccs = []
    for t in range(nt):
        xacc = jnp.broadcast_to(sb_ref[...], (sb, SSM_XBC))
        for k in range(SSM_CONV_WIDTH):
            xacc = xacc + sw_ref[k:k + 1, :] * ssm_ext(t + k)
        xaccs.append(xacc)
    xbc = _silu(jnp.concatenate(xaccs, axis=0))
    xs = xbc[:, 0:SSM_DIM]
    bmf = xbc[:, SSM_DIM:SSM_DIM + SSM_BC]
    cmf = xbc[:, SSM_DIM + SSM_BC:]
    b_out[...] = bmf.reshape(nt, sb, SSM_BC)
    c_out[...] = cmf.reshape(nt, sb, SSM_BC)
    bm = bmf.astype(BF16)
    cm = cmf.astype(BF16)

    dt = _softplus(_dot(hb, wdt_ref[...]) + dtb_ref[...])
    dta = dt * (-jnp.exp(alog_ref[...]))
    cums = [dta[0:sb, :]]
    for t in range(1, nt):
        cums.append(cums[-1] + dta[t * sb:(t + 1) * sb, :])
    cum = jnp.concatenate(cums, axis=0)
    cum_last = tile_rows(cums[-1])
    dec_out[...] = jnp.exp(cums[-1])
    cum_t = cum.T

    fac = jnp.concatenate([dt, jnp.exp(cum_last - cum), jnp.exp(cum)], axis=0)
    facx = _exact_right(fac, e3_ref[...])
    dt_x = facx[0:rows]
    dend_x = facx[rows:2 * rows]
    ec_out[...] = facx[2 * rows:3 * rows].reshape(nt, sb, SSM_DIM)
    xdt = xs * dt_x
    xdt_b = xdt.astype(BF16)
    xd_out[...] = (xdt * dend_x).reshape(nt, sb, SSM_DIM)

    row = lax.broadcasted_iota(jnp.int32, (rows, rows), 0)
    col = lax.broadcasted_iota(jnp.int32, (rows, rows), 1)
    mask = (col <= row) & ((col % sb) == (row % sb))
    ys = []
    for g in range(SSM_GROUPS):
        gs = slice(g * GROUP_DIM, (g + 1) * GROUP_DIM)
        ns = slice(g * SSM_STATE, (g + 1) * SSM_STATE)
        ys.append(_ssd_intra_group(cm[:, ns], bm[:, ns], cum, cum_t, mask, xdt_b[:, gs], g))
    yp = jnp.concatenate(ys, axis=1) + dexp_ref[...] * xs
    yp_out[...] = yp.reshape(nt, sb, SSM_DIM)


def _sample_pre_call(x_pm, mod, cst_t, sst_t, p):
    nt, nb, _ = x_pm.shape
    sb = SAMPLE_SEQ_BLOCK
    consts = (p["g_pre"], p["cw"], p["cb"], p["lng"], p["lnb"], p["sw"], p["sb"], p["dtb"], p["alog"],
              p["dexp"], p["e3"], p["w_v"], p["w_x"], p["w_dt"], p["w_co"])

    def pm(width, n=nt):
        return pl.BlockSpec((n, sb, width), lambda i: (0, i, 0))

    in_specs = ([pm(D_MODEL),
                 pl.BlockSpec((sb, D_MODEL), lambda i: (i, 0)),
                 pl.BlockSpec((sb, D_MODEL), lambda i: (i, 1)),
                 pm(D_MODEL, CONV_WIDTH - 1), pm(SSM_XBC, SSM_CONV_WIDTH - 1)]
                + [_const(c.shape, 1) for c in consts])

    def sds(n, width, dtype=F32):
        return jax.ShapeDtypeStruct((n, nb, width), dtype)

    out_shape = (sds(nt, D_MODEL, BF16), sds(nt, D_MODEL), sds(nt, D_MODEL),
                 sds(SSM_CONV_WIDTH - 1, SSM_XBC), sds(nt, SSM_DIM), sds(nt, SSM_DIM),
                 sds(nt, SSM_BC), sds(nt, SSM_BC), sds(nt, SSM_DIM),
                 jax.ShapeDtypeStruct((nb, LANES), F32))
    out_specs = (pm(D_MODEL), pm(D_MODEL), pm(D_MODEL), pm(SSM_XBC, SSM_CONV_WIDTH - 1), pm(SSM_DIM),
                 pm(SSM_DIM), pm(SSM_BC), pm(SSM_BC), pm(SSM_DIM),
                 pl.BlockSpec((sb, LANES), lambda i: (i, 0)))
    return pl.pallas_call(
        _sample_pre_kernel,
        grid=(nb // sb,),
        in_specs=in_specs,
        out_specs=out_specs,
        out_shape=out_shape,
        compiler_params=pltpu.CompilerParams(
            dimension_semantics=("arbitrary",), vmem_limit_bytes=VMEM_LIMIT),
        name="sample_pre",
    )(x_pm, mod, mod, cst_t, sst_t, *consts)


def _sample_state_kernel(dec_ref, st_ref, c_ref, b_ref, xd_ref, st_out, yi_out):
    nt, sb = c_ref.shape[0], c_ref.shape[1]
    rows = nt * sb
    i = pl.program_id(0)
    cmat = c_ref[...].reshape(rows, SSM_BC).astype(BF16)
    bmat = b_ref[...].reshape(rows, SSM_BC)
    xd = xd_ref[...].reshape(rows, SSM_DIM)
    seq_of_row = lax.broadcasted_iota(jnp.int32, (rows, GROUP_DIM), 0) % sb
    seq_of_row_n = lax.broadcasted_iota(jnp.int32, (rows, SSM_STATE), 0) % sb
    accs = []
    for g in range(SSM_GROUPS):
        gs = slice(g * GROUP_DIM, (g + 1) * GROUP_DIM)
        ns = slice(g * SSM_STATE, (g + 1) * SSM_STATE)
        cg = cmat[:, ns]
        bg = bmat[:, ns]
        xd_t = xd[:, gs].T.astype(BF16)
        acc = jnp.zeros((rows, GROUP_DIM), F32)
        for b in range(sb):
            mine = seq_of_row == b
            h0 = st_ref[b, gs, :]
            acc = jnp.where(mine, _dot_nt(cg, h0.astype(BF16)), acc)
            b_mine = jnp.where(seq_of_row_n == b, bg, 0.0).astype(BF16)
            st = _dot(xd_t, b_mine)
            for r in range(HEADS_PER_GROUP):
                hs = slice(r * SSM_HEAD_DIM, (r + 1) * SSM_HEAD_DIM)
                dec = dec_ref[i * sb + b, g * HEADS_PER_GROUP + r]
                st_out[b, g * GROUP_DIM + r * SSM_HEAD_DIM:g * GROUP_DIM + (r + 1) * SSM_HEAD_DIM, :] = (
                    dec * h0[hs, :] + st[hs, :])
        accs.append(acc)
    yi_out[...] = jnp.concatenate(accs, axis=1).reshape(nt, sb, SSM_DIM)


def _sample_state_call(dec, state, c_pm, b_pm, xd_pm):
    nt, nb, _ = c_pm.shape
    sb = STATE_SEQ_BLOCK

    def pm(width):
        return pl.BlockSpec((nt, sb, width), lambda i: (0, i, 0))

    st_spec = pl.BlockSpec((sb, SSM_DIM, SSM_STATE), lambda i: (i, 0, 0))
    return pl.pallas_call(
        _sample_state_kernel,
        grid=(nb // sb,),
        in_specs=[pl.BlockSpec(memory_space=pltpu.SMEM), st_spec, pm(SSM_BC), pm(SSM_BC), pm(SSM_DIM)],
        out_specs=(st_spec, pm(SSM_DIM)),
        out_shape=(jax.ShapeDtypeStruct(state.shape, F32), jax.ShapeDtypeStruct((nt, nb, SSM_DIM), F32)),
        compiler_params=pltpu.CompilerParams(
            dimension_semantics=("arbitrary",), vmem_limit_bytes=VMEM_LIMIT),
        name="sample_state",
    )(dec, state, c_pm, b_pm, xd_pm)


def _sample_post_kernel(x_ref, gt_ref, h_ref, yp_ref, yi_ref, ec_ref, bc_ref, ng_ref, gpost_ref,
                        wz_ref, wm_ref, wso_ref, wo_ref, y_out):
    reps = x_ref.shape[0] // gt_ref.shape[0]
    gate = jnp.concatenate([gt_ref[...]] * reps, axis=0)
    hb = h_ref[...]
    y = (yp_ref[...] + yi_ref[...] * ec_ref[...]) * _silu(_dot(hb, wz_ref[...]))
    branch_ssm = _dot(_group_rmsnorm(y, ng_ref[...]).astype(BF16), wso_ref[...])
    m = _dot(hb, wm_ref[...])
    merged = _sig(m[:, 0:D_MODEL]) * bc_ref[...] + _sig(m[:, D_MODEL:]) * branch_ssm
    o = _dot(merged.astype(BF16), wo_ref[...])
    y_out[...] = _final(x_ref[...], gate, o, gpost_ref[...])


def _sample_post_call(x2, mod, h2, yp2, yi2, ec2, bc2, p):
    n = x2.shape[0]
    nb = mod.shape[0]
    rb = POST_ROWS
    consts = (p["ng"], p["g_post"], p["w_z"], p["w_m"], p["w_so"], p["w_o"])

    def rowblk(width):
        return pl.BlockSpec((rb, width), lambda i: (i, 0))

    in_specs = ([rowblk(D_MODEL), pl.BlockSpec((nb, D_MODEL), lambda i: (0, 2)), rowblk(D_MODEL),
                 rowblk(SSM_DIM), rowblk(SSM_DIM), rowblk(SSM_DIM), rowblk(D_MODEL)]
                + [_const(c.shape, 1) for c in consts])
    return pl.pallas_call(
        _sample_post_kernel,
        grid=(n // rb,),
        in_specs=in_specs,
        out_specs=rowblk(D_MODEL),
        out_shape=jax.ShapeDtypeStruct((n, D_MODEL), F32),
        compiler_params=pltpu.CompilerParams(
            dimension_semantics=("arbitrary",), vmem_limit_bytes=VMEM_LIMIT),
        name="sample_post",
    )(x2, mod, h2, yp2, yi2, ec2, bc2, *consts)


def _conf_roll_kernel(st_ref, u_ref, o_ref):
    keep = (CONV_WIDTH - 1) - u_ref.shape[1]
    o_ref[:, 0:keep, :] = st_ref[:, u_ref.shape[1]:, :]
    o_ref[:, keep:, :] = u_ref[...]


def _conf_roll_call(state, u_new):
    nb, hist, width = state.shape
    nt = u_new.shape[1]
    sb = STATE_SEQ_BLOCK
    return pl.pallas_call(
        _conf_roll_kernel,
        grid=(nb // sb,),
        in_specs=[pl.BlockSpec((sb, hist, width), lambda i: (i, 0, 0)),
                  pl.BlockSpec((sb, nt, width), lambda i: (i, 0, 0))],
        out_specs=pl.BlockSpec((sb, hist, width), lambda i: (i, 0, 0)),
        out_shape=jax.ShapeDtypeStruct(state.shape, F32),
        name="conf_roll",
    )(state, u_new)


def _selection_constants():
    tri = np.tril(np.ones((SSM_CHUNK, SSM_CHUNK), np.float32))
    tri3 = np.concatenate([tri, tri, tri], axis=1)
    e = np.zeros((LANES, SSM_DIM), np.float32)
    for hd in range(SSM_HEADS):
        e[hd, hd * SSM_HEAD_DIM:(hd + 1) * SSM_HEAD_DIM] = 1.0
    e3 = np.concatenate([e, e, e], axis=0)
    return jnp.asarray(tri3, BF16), jnp.asarray(e3, BF16)


def _pad_lanes(v, n=LANES):
    return jnp.pad(v, ((0, 0), (0, n - v.shape[1])))


def kernel(x_prompt, x_sample, c_prompt, c_sample, state_conf_conv, state_ssm_conv, state_ssm, w_ada, b_ada, g_pre, g_post, w_in, conf_dw_w, conf_dw_b, conf_ln_g, conf_ln_b, w_conf_out, ssm_dw_w, ssm_dw_b, ssm_dt_bias, ssm_a_log, ssm_d, ssm_norm_g, w_ssm_out, w_o):
    bp = x_prompt.shape[0]
    nb, nt, _ = x_sample.shape
    l = 0
    mod = _ada(jnp.concatenate([c_prompt, c_sample], axis=0), w_ada[l], b_ada[l][None])

    wi = w_in[l]
    o_z = 3 * D_MODEL
    o_x = o_z + SSM_DIM
    o_dt = o_x + SSM_XBC
    o_m = o_dt + SSM_HEADS
    tri3, e3 = _selection_constants()
    p = {
        "w_v": wi[:, :o_z].astype(BF16), "w_z": wi[:, o_z:o_x].astype(BF16), "w_x": wi[:, o_x:o_dt].astype(BF16),
        "w_dt": _pad_lanes(wi[:, o_dt:o_m]).astype(BF16), "w_m": wi[:, o_m:].astype(BF16),
        "w_co": w_conf_out[l].astype(BF16), "w_so": w_ssm_out[l].astype(BF16), "w_o": w_o[l].astype(BF16),
        "g_pre": g_pre[l][None], "g_post": g_post[l][None],
        "cw": jnp.pad(conf_dw_w[l], ((0, 1), (0, 0))), "cb": conf_dw_b[l][None],
        "lng": conf_ln_g[l][None], "lnb": conf_ln_b[l][None],
        "sw": ssm_dw_w[l], "sb": ssm_dw_b[l][None],
        "dtb": _pad_lanes(ssm_dt_bias[l][None]), "alog": _pad_lanes(ssm_a_log[l][None]),
        "dexp": jnp.repeat(ssm_d[l], SSM_HEAD_DIM)[None], "ng": ssm_norm_g[l][None],
        "tri3": tri3, "e3": e3,
    }

    yp, conf_p, ssmc_p, st_p = _prompt_call(x_prompt, mod[:bp], p)
    st_p = st_p.reshape(1, bp, SSM_HEADS, SSM_HEAD_DIM, SSM_STATE)

    mod_s = mod[bp:]
    x_pm = x_sample.transpose(1, 0, 2)
    cst_t = state_conf_conv[l].transpose(1, 0, 2)
    sst_t = state_ssm_conv[l].transpose(1, 0, 2)
    h_pm, bc_pm, unew_pm, xnew_pm, yp_pm, ec_pm, c_pm, b_pm, xd_pm, dec = _sample_pre_call(
        x_pm, mod_s, cst_t, sst_t, p)
    st_s, yi_pm = _sample_state_call(dec[:, :SSM_HEADS], state_ssm[l].reshape(nb, SSM_DIM, SSM_STATE),
                                     c_pm, b_pm, xd_pm)

    def flat(v):
        return v.reshape(nt * nb, v.shape[-1])

    ys = _sample_post_call(flat(x_pm), mod_s, flat(h_pm), flat(yp_pm), flat(yi_pm), flat(ec_pm), flat(bc_pm), p)
    ys = ys.reshape(nt, nb, D_MODEL).transpose(1, 0, 2)
    conf_s = _conf_roll_call(state_conf_conv[l], unew_pm.transpose(1, 0, 2))
    ssmc_s = xnew_pm.transpose(1, 0, 2)
    st_s = st_s.reshape(1, nb, SSM_HEADS, SSM_HEAD_DIM, SSM_STATE)
    return (yp, ys, conf_p[None], ssmc_p[None], st_p, conf_s[None], ssmc_s[None], st_s)
```

```python
import functools

import jax
import jax.numpy as jnp
import numpy as np
from jax import lax
from jax.experimental import pallas as pl
from jax.experimental.pallas import tpu as pltpu

F32 = jnp.float32
BF16 = jnp.bfloat16

D_MODEL = 1024
CONV_WIDTH = 31
SSM_DIM = 2048
SSM_HEADS = 32
SSM_HEAD_DIM = 64
SSM_GROUPS = 8
SSM_STATE = 128
SSM_CONV_WIDTH = 4
SSM_BC = SSM_GROUPS * SSM_STATE
SSM_XBC = SSM_DIM + 2 * SSM_BC
SSM_CHUNK = 128
GROUP_DIM = SSM_DIM // SSM_GROUPS
HEADS_PER_GROUP = SSM_HEADS // SSM_GROUPS
EPS = 1e-6

SUBLANES = 8
LANES = 128
VMEM_LIMIT = 60 * 1024 * 1024

ZM_COLS = 512
ZM_CHUNKS = 2 * SSM_DIM // ZM_COLS

PROMPT_TILE = 256
CONF_HALO = 32
SSM_HALO = 8

SAMPLE_SEQ_BLOCK = 32
STATE_SEQ_BLOCK = 8
POST_ROWS = 256


def _sig(x):
    return 0.5 * jnp.tanh(0.5 * x) + 0.5


def _silu(x):
    return x * _sig(x)


def _dot(a, b):
    return jnp.dot(a, b, preferred_element_type=F32)


def _dot_nt(a, b):
    return lax.dot_general(a, b, (((1,), (1,)), ((), ())), preferred_element_type=F32)


def _dot_tn(a, b):
    return lax.dot_general(a, b, (((0,), (0,)), ((), ())), preferred_element_type=F32)


def _split3(v):
    hi = v.astype(BF16)
    r1 = v - hi.astype(F32)
    mid = r1.astype(BF16)
    lo = (r1 - mid.astype(F32)).astype(BF16)
    return hi, mid, lo


def _exact_left(sel3, v):
    hi, mid, lo = _split3(v)
    return _dot(sel3, jnp.concatenate([hi, mid, lo], axis=0))


def _expand_heads(v, e3):
    lane = lax.broadcasted_iota(jnp.int32, v.shape, 1)
    v = jnp.where(lane < SSM_HEADS, v, 0.0)
    hi = v.astype(BF16).astype(F32)
    r1 = v - hi
    mid = r1.astype(BF16).astype(F32)
    packed = hi + pltpu.roll(mid, SSM_HEADS, 1) + pltpu.roll(r1 - mid, 2 * SSM_HEADS, 1)
    return _dot(packed.astype(BF16), e3)


def _softplus(x):
    return jnp.maximum(x, 0.0) + jnp.log1p(jnp.exp(-jnp.abs(x)))


def _prenorm(x, g_pre, scale, shift):
    r = lax.rsqrt(jnp.mean(x * x, axis=-1, keepdims=True) + EPS)
    return ((x * r) * g_pre) * (1.0 + scale) + shift


def _layernorm(x, g, b):
    mu = jnp.mean(x, axis=-1, keepdims=True)
    xc = x - mu
    var = jnp.mean(xc * xc, axis=-1, keepdims=True)
    return (xc * lax.rsqrt(var + EPS)) * g + b


def _group_rmsnorm(y, g):
    outs = []
    for gi in range(SSM_GROUPS):
        yg = y[:, gi * GROUP_DIM:(gi + 1) * GROUP_DIM]
        r = lax.rsqrt(jnp.mean(yg * yg, axis=-1, keepdims=True) + EPS)
        outs.append((yg * r) * g[:, gi * GROUP_DIM:(gi + 1) * GROUP_DIM])
    return jnp.concatenate(outs, axis=1)


def _final(x, gate, o, g_post):
    r = lax.rsqrt(jnp.mean(o * o, axis=-1, keepdims=True) + EPS)
    return x + gate * ((o * r) * g_post)


CONV_ROW_STRIDE = 4
CONV_WINDOW = 128


def _to_slabs(buf, v, row0, slab0):
    for s in range(v.shape[1] // LANES):
        buf[slab0 + s, row0:row0 + v.shape[0], :] = v[:, s * LANES:(s + 1) * LANES]


def _conv_slab(buf, w_ref, b_ref, out, s, width, halo, rows, act):
    base = halo - (width - 1)
    n = CONV_WINDOW // CONV_ROW_STRIDE
    for w0 in range(0, rows, CONV_WINDOW):
        accs = [jnp.broadcast_to(b_ref[s], (n, LANES)) for _ in range(CONV_ROW_STRIDE)]
        for d in range(width + CONV_ROW_STRIDE - 1):
            v = buf[s, pl.ds(base + w0 + d, n, stride=CONV_ROW_STRIDE), :]
            for o in range(CONV_ROW_STRIDE):
                if 0 <= d - o < width:
                    accs[o] = accs[o] + jnp.broadcast_to(w_ref[s, d - o:d - o + 1, :], (n, LANES)) * v
        for o in range(CONV_ROW_STRIDE):
            out[s, pl.ds(w0 + o, n, stride=CONV_ROW_STRIDE), :] = act(accs[o])


def _carry_slabs(buf, tail_out, width, halo, rows):
    for s in range(buf.shape[0]):
        tail_out[:, s * LANES:(s + 1) * LANES] = buf[s, rows + halo - (width - 1):rows + halo, :]
        buf[s, 0:halo, :] = buf[s, rows:rows + halo, :]


def _ssd_intra_group(cg, bg, cum, cum_t, mask, xdt_g, g):
    rows = cg.shape[0]
    cb = _dot_nt(cg, bg)
    lane = lax.broadcasted_iota(jnp.int32, (rows, GROUP_DIM), 1)
    ms, xbd = [], []
    for r in range(HEADS_PER_GROUP):
        hd = g * HEADS_PER_GROUP + r
        seg = cum[:, hd:hd + 1] - cum_t[hd:hd + 1, :]
        dec = jnp.exp(jnp.where(mask, seg, -jnp.inf))
        ms.append((cb * dec).astype(BF16))
        in_head = (lane >= r * SSM_HEAD_DIM) & (lane < (r + 1) * SSM_HEAD_DIM)
        xbd.append(jnp.where(in_head, xdt_g, jnp.zeros_like(xdt_g)))
    return _dot(jnp.concatenate(ms, axis=1), jnp.concatenate(xbd, axis=0))


def _ada_kernel(c_ref, w_ref, b_ref, o_ref):
    s = _silu(c_ref[...]).astype(BF16)
    o_ref[...] = _dot(s, w_ref[...].astype(BF16)) + b_ref[...]


def _ada(c_all, w_ada, b_ada):
    n = c_all.shape[0]
    return pl.pallas_call(
        _ada_kernel,
        grid=(3,),
        in_specs=[
            pl.BlockSpec((n, D_MODEL), lambda k: (0, 0)),
            pl.BlockSpec((D_MODEL, D_MODEL), lambda k: (0, k)),
            pl.BlockSpec((1, D_MODEL), lambda k: (0, k)),
        ],
        out_specs=pl.BlockSpec((n, D_MODEL), lambda k: (0, k)),
        out_shape=jax.ShapeDtypeStruct((n, 3 * D_MODEL), F32),
        name="ada_mod",
    )(c_all, w_ada, b_ada)


def _prompt_kernel(x_ref, sh_ref, sc_ref, gt_ref, gpre_ref, gpost_ref, cw_ref, cb_ref, lng_ref, lnb_ref,
                   sw_ref, sb_ref, dtb_ref, alog_ref, dexp_ref, ng_ref, tri_ref, e3_ref,
                   wv_ref, wx_ref, wzm_ref, wdt_ref, wco_ref, wso_ref, wo_ref,
                   y_out, conf_out, ssmc_out, st_out,
                   h_ref, ubuf, cacc, xbuf, xact, ht_ref, zm_ref, yn_ref):
    T = PROMPT_TILE
    j = pl.program_id(1)

    @pl.when(j == 0)
    def _():
        ubuf[:, 0:CONF_HALO, :] = jnp.zeros((D_MODEL // LANES, CONF_HALO, LANES), F32)
        xbuf[:, 0:SSM_HALO, :] = jnp.zeros((SSM_XBC // LANES, SSM_HALO, LANES), F32)
        ht_ref[...] = jnp.zeros_like(ht_ref)

    h_ref[...] = _prenorm(x_ref[...], gpre_ref[...], sc_ref[...], sh_ref[...]).astype(BF16)

    u = _dot(h_ref[...], wv_ref[:, 0:D_MODEL]) * _sig(_dot(h_ref[...], wv_ref[:, D_MODEL:2 * D_MODEL]))
    _to_slabs(ubuf, u, CONF_HALO, 0)
    c_slabs = D_MODEL // LANES
    spc = ZM_COLS // LANES
    assert c_slabs == ZM_CHUNKS == SSM_XBC // ZM_COLS

    def project_xbc(i):
        _to_slabs(xbuf, _dot(h_ref[...], wx_ref[i]), SSM_HALO, i * spc)

    project_xbc(0)
    for i in range(ZM_CHUNKS):
        _conv_slab(ubuf, cw_ref, cb_ref, cacc, i, CONV_WIDTH, CONF_HALO, T, lambda v: v)
        for q in range(spc):
            _conv_slab(xbuf, sw_ref, sb_ref, xact, i * spc + q, SSM_CONV_WIDTH, SSM_HALO, T, _silu)
        if i + 1 < ZM_CHUNKS:
            project_xbc(i + 1)
        zm_ref[i] = (_silu if i < ZM_CHUNKS // 2 else _sig)(_dot(h_ref[...], wzm_ref[i]))
    _carry_slabs(ubuf, conf_out, CONV_WIDTH, CONF_HALO, T)
    _carry_slabs(xbuf, ssmc_out, SSM_CONV_WIDTH, SSM_HALO, T)
    acc = jnp.concatenate([cacc[s] for s in range(c_slabs)], axis=1)
    uc = _silu(_layernorm(acc, lng_ref[...], lnb_ref[...])) * _silu(
        _dot(h_ref[...], wv_ref[:, 2 * D_MODEL:3 * D_MODEL]))
    branch_conv = _dot(uc.astype(BF16), wco_ref[...])
    xs_slabs = SSM_DIM // LANES
    b_slab0 = xs_slabs
    c_slab0 = xs_slabs + SSM_GROUPS
    slabs_per_group = GROUP_DIM // LANES

    dt = _softplus(_dot(h_ref[...], wdt_ref[...]) + dtb_ref[...])
    dta = dt * (-jnp.exp(alog_ref[...]))

    row = lax.broadcasted_iota(jnp.int32, (SSM_CHUNK, SSM_CHUNK), 0)
    col = lax.broadcasted_iota(jnp.int32, (SSM_CHUNK, SSM_CHUNK), 1)
    causal = col <= row

    for c in range(T // SSM_CHUNK):
        rs = slice(c * SSM_CHUNK, (c + 1) * SSM_CHUNK)
        cum = _exact_left(tri_ref[...], dta[rs, :])
        cum_last = cum[SSM_CHUNK - 1:SSM_CHUNK, :]
        cum_t = cum.T
        fac = jnp.concatenate([
            dt[rs, :],
            jnp.exp(cum_last - cum),
            jnp.exp(cum),
            jnp.broadcast_to(jnp.exp(cum_last), (SUBLANES, LANES)),
        ], axis=0)
        facx = _expand_heads(fac, e3_ref[...])
        dt_x = facx[0:SSM_CHUNK]
        dend_x = facx[SSM_CHUNK:2 * SSM_CHUNK]
        ecum_x = facx[2 * SSM_CHUNK:3 * SSM_CHUNK]
        cdec_x = facx[3 * SSM_CHUNK:3 * SSM_CHUNK + 1]
        xdt = jnp.concatenate([xact[s, rs, :] for s in range(xs_slabs)], axis=1) * dt_x
        xdt_b = xdt.astype(BF16)
        xdte_b = (xdt * dend_x).astype(BF16)
        for g in range(SSM_GROUPS):
            gs = slice(g * GROUP_DIM, (g + 1) * GROUP_DIM)
            ns = slice(g * SSM_STATE, (g + 1) * SSM_STATE)
            bg = xact[b_slab0 + g, rs, :].astype(BF16)
            cg = xact[c_slab0 + g, rs, :].astype(BF16)
            htg = ht_ref[:, gs]
            y_g = _dot(cg, htg.astype(BF16)) * ecum_x[:, gs]
            y_g = y_g + _ssd_intra_group(cg, bg, cum, cum_t, causal, xdt_b[:, gs], g)
            _to_slabs(xbuf, y_g, SSM_HALO + c * SSM_CHUNK, g * slabs_per_group)
            ht_ref[:, gs] = cdec_x[:, gs] * htg + _dot_tn(bg, xdte_b[:, gs])

    for g in range(SSM_GROUPS):
        gs = slice(g * GROUP_DIM, (g + 1) * GROUP_DIM)
        sl = range(g * slabs_per_group, (g + 1) * slabs_per_group)
        y_g = jnp.concatenate([xbuf[s, SSM_HALO:SSM_HALO + T, :] for s in sl], axis=1)
        xs_g = jnp.concatenate([xact[s] for s in sl], axis=1)
        zc = g * GROUP_DIM % ZM_COLS
        zs_g = zm_ref[g * GROUP_DIM // ZM_COLS, :, zc:zc + GROUP_DIM]
        y_g = (y_g + dexp_ref[:, gs] * xs_g) * zs_g
        r = lax.rsqrt(jnp.mean(y_g * y_g, axis=-1, keepdims=True) + EPS)
        yn_ref[:, gs] = ((y_g * r) * ng_ref[:, gs]).astype(BF16)
    branch_ssm = _dot(yn_ref[...], wso_ref[...])

    m0 = SSM_DIM // ZM_COLS
    per = D_MODEL // ZM_COLS
    mg_conv = jnp.concatenate([zm_ref[m0 + q] for q in range(per)], axis=1)
    mg_ssm = jnp.concatenate([zm_ref[m0 + per + q] for q in range(per)], axis=1)
    merged = mg_conv * branch_conv + mg_ssm * branch_ssm
    o = _dot(merged.astype(BF16), wo_ref[...])
    y_out[...] = _final(x_ref[...], gt_ref[...], o, gpost_ref[...])

    @pl.when(j == pl.num_programs(1) - 1)
    def _():
        st_out[...] = ht_ref[...].T


def _const(shape, ngrid):
    nd = len(shape)
    return pl.BlockSpec(shape, lambda *_: (0,) * nd, pipeline_mode=pl.Buffered(1))


def _prompt_call(x, mod, p):
    bsz, seq, _ = x.shape
    T = PROMPT_TILE
    nt = seq // T
    mod3 = mod.reshape(bsz, 1, 3 * D_MODEL)
    consts = (p["g_pre"], p["g_post"], p["cw_s"], p["cb_s"], p["lng"], p["lnb"], p["sw_s"], p["sb_s"], p["dtb"],
              p["alog"], p["dexp"], p["ng"], p["tri3"], p["e3"],
              p["w_v"], p["w_x"], p["w_zm"], p["w_dt"], p["w_co"], p["w_so"], p["w_o"])
    mod_specs = [pl.BlockSpec((None, 1, D_MODEL), functools.partial(lambda k, b, j: (b, 0, k), k))
                 for k in range(3)]
    in_specs = ([pl.BlockSpec((None, T, D_MODEL), lambda b, j: (b, j, 0))] + mod_specs
                + [_const(c.shape, 2) for c in consts])
    out_shape = (
        jax.ShapeDtypeStruct((bsz, seq, D_MODEL), F32),
        jax.ShapeDtypeStruct((bsz, CONV_WIDTH - 1, D_MODEL), F32),
        jax.ShapeDtypeStruct((bsz, SSM_CONV_WIDTH - 1, SSM_XBC), F32),
        jax.ShapeDtypeStruct((bsz, SSM_DIM, SSM_STATE), F32),
    )
    out_specs = (
        pl.BlockSpec((None, T, D_MODEL), lambda b, j: (b, j, 0)),
        pl.BlockSpec((None, CONV_WIDTH - 1, D_MODEL), lambda b, j: (b, 0, 0)),
        pl.BlockSpec((None, SSM_CONV_WIDTH - 1, SSM_XBC), lambda b, j: (b, 0, 0)),
        pl.BlockSpec((None, SSM_DIM, SSM_STATE), lambda b, j: (b, 0, 0)),
    )
    scratch = [
        pltpu.VMEM((T, D_MODEL), BF16),
        pltpu.VMEM((D_MODEL // LANES, T + CONF_HALO, LANES), F32),
        pltpu.VMEM((D_MODEL // LANES, T, LANES), F32),
        pltpu.VMEM((SSM_XBC // LANES, T + SSM_HALO, LANES), F32),
        pltpu.VMEM((SSM_XBC // LANES, T, LANES), F32),
        pltpu.VMEM((SSM_STATE, SSM_DIM), F32),
        pltpu.VMEM((ZM_CHUNKS, T, ZM_COLS), F32),
        pltpu.VMEM((T, SSM_DIM), BF16),
    ]
    return pl.pallas_call(
        _prompt_kernel,
        grid=(bsz, nt),
        in_specs=in_specs,
        out_specs=out_specs,
        out_shape=out_shape,
        scratch_shapes=scratch,
        compiler_params=pltpu.CompilerParams(
            dimension_semantics=("arbitrary", "arbitrary"), vmem_limit_bytes=VMEM_LIMIT),
        name="prompt_layer",
    )(x, mod3, mod3, mod3, *consts)


def _sample_pre_kernel(x_ref, sh_ref, sc_ref, cst_ref, sst_ref,
                       gpre_ref, cw_ref, cb_ref, lng_ref, lnb_ref, sw_ref, sb_ref, dtb_ref, alog_ref,
                       dexp_ref, e3_ref, wv_ref, wx_ref, wdt_ref, wco_ref,
                       h_out, bc_out, unew_out, xnew_out, yp_out, ec_out, c_out, b_out, xd_out, dec_out):
    nt, sb = x_ref.shape[0], x_ref.shape[1]
    rows = nt * sb

    def tile_rows(v):
        return jnp.concatenate([v] * nt, axis=0)

    x = x_ref[...].reshape(rows, D_MODEL)
    hb = _prenorm(x, gpre_ref[...], tile_rows(sc_ref[...]), tile_rows(sh_ref[...])).astype(BF16)
    h_out[...] = hb.reshape(nt, sb, D_MODEL)

    u = _dot(hb, wv_ref[:, 0:D_MODEL]) * _sig(_dot(hb, wv_ref[:, D_MODEL:2 * D_MODEL]))
    unew_out[...] = u.reshape(nt, sb, D_MODEL)

    def conf_ext(jj):
        if jj < CONV_WIDTH - 1:
            return cst_ref[jj]
        return u[(jj - (CONV_WIDTH - 1)) * sb:(jj - (CONV_WIDTH - 2)) * sb, :]

    accs = []
    for t in range(nt):
        acc = jnp.broadcast_to(cb_ref[...], (sb, D_MODEL))
        for k in range(CONV_WIDTH):
            acc = acc + cw_ref[k:k + 1, :] * conf_ext(t + k)
        accs.append(acc)
    acc = jnp.concatenate(accs, axis=0)
    uc = _silu(_layernorm(acc, lng_ref[...], lnb_ref[...])) * _silu(_dot(hb, wv_ref[:, 2 * D_MODEL:3 * D_MODEL]))
    bc_out[...] = _dot(uc.astype(BF16), wco_ref[...]).reshape(nt, sb, D_MODEL)

    xpre = jnp.concatenate([_dot(hb, wx_ref[c]) for c in range(wx_ref.shape[0])], axis=1)
    xnew_out[...] = xpre[(nt - (SSM_CONV_WIDTH - 1)) * sb:, :].reshape(SSM_CONV_WIDTH - 1, sb, SSM_XBC)

    def ssm_ext(jj):
        if jj < SSM_CONV_WIDTH - 1:
            return sst_ref[jj]
        return xpre[(jj - (SSM_CONV_WIDTH - 1)) * sb:(jj - (SSM_CONV_WIDTH - 2)) * sb, :]

    xaccs = []
    for t in range(nt):
        xacc = jnp.broadcast_to(sb_ref[...], (sb, SSM_XBC))
        for k in range(SSM_CONV_WIDTH):
            xacc = xacc + sw_ref[k:k + 1, :] * ssm_ext(t + k)
        xaccs.append(xacc)
    xbc = _silu(jnp.concatenate(xaccs, axis=0))
    xs = xbc[:, 0:SSM_DIM]
    bmf = xbc[:, SSM_DIM:SSM_DIM + SSM_BC]
    cmf = xbc[:, SSM_DIM + SSM_BC:]
    b_out[...] = bmf.reshape(nt, sb, SSM_BC)
    c_out[...] = cmf.reshape(nt, sb, SSM_BC)
    bm = bmf.astype(BF16)
    cm = cmf.astype(BF16)

    dt = _softplus(_dot(hb, wdt_ref[...]) + dtb_ref[...])
    dta = dt * (-jnp.exp(alog_ref[...]))
    cums = [dta[0:sb, :]]
    for t in range(1, nt):
        cums.append(cums[-1] + dta[t * sb:(t + 1) * sb, :])
    cum = jnp.concatenate(cums, axis=0)
    cum_last = tile_rows(cums[-1])
    dec_out[...] = jnp.exp(cums[-1])
    cum_t = cum.T

    fac = jnp.concatenate([dt, jnp.exp(cum_last - cum), jnp.exp(cum)], axis=0)
    facx = _expand_heads(fac, e3_ref[...])
    dt_x = facx[0:rows]
    dend_x = facx[rows:2 * rows]
    ec_out[...] = facx[2 * rows:3 * rows].reshape(nt, sb, SSM_DIM)
    xdt = xs * dt_x
    xdt_b = xdt.astype(BF16)
    xd_out[...] = (xdt * dend_x).reshape(nt, sb, SSM_DIM)

    row = lax.broadcasted_iota(jnp.int32, (rows, rows), 0)
    col = lax.broadcasted_iota(jnp.int32, (rows, rows), 1)
    mask = (col <= row) & ((col % sb) == (row % sb))
    ys = []
    for g in range(SSM_GROUPS):
        gs = slice(g * GROUP_DIM, (g + 1) * GROUP_DIM)
        ns = slice(g * SSM_STATE, (g + 1) * SSM_STATE)
        ys.append(_ssd_intra_group(cm[:, ns], bm[:, ns], cum, cum_t, mask, xdt_b[:, gs], g))
    yp = jnp.concatenate(ys, axis=1) + dexp_ref[...] * xs
    yp_out[...] = yp.reshape(nt, sb, SSM_DIM)


def _sample_pre_call(x_pm, mod, cst_t, sst_t, p):
    nt, nb, _ = x_pm.shape
    sb = SAMPLE_SEQ_BLOCK
    consts = (p["g_pre"], p["cw"], p["cb"], p["lng"], p["lnb"], p["sw"], p["sb"], p["dtb"], p["alog"],
              p["dexp"], p["e3"], p["w_v"], p["w_x"], p["w_dt"], p["w_co"])

    def pm(width, n=nt):
        return pl.BlockSpec((n, sb, width), lambda i: (0, i, 0))

    in_specs = ([pm(D_MODEL),
                 pl.BlockSpec((sb, D_MODEL), lambda i: (i, 0)),
                 pl.BlockSpec((sb, D_MODEL), lambda i: (i, 1)),
                 pm(D_MODEL, CONV_WIDTH - 1), pm(SSM_XBC, SSM_CONV_WIDTH - 1)]
                + [_const(c.shape, 1) for c in consts])

    def sds(n, width, dtype=F32):
        return jax.ShapeDtypeStruct((n, nb, width), dtype)

    out_shape = (sds(nt, D_MODEL, BF16), sds(nt, D_MODEL), sds(nt, D_MODEL),
                 sds(SSM_CONV_WIDTH - 1, SSM_XBC), sds(nt, SSM_DIM), sds(nt, SSM_DIM),
                 sds(nt, SSM_BC), sds(nt, SSM_BC), sds(nt, SSM_DIM),
                 jax.ShapeDtypeStruct((nb, LANES), F32))
    out_specs = (pm(D_MODEL), pm(D_MODEL), pm(D_MODEL), pm(SSM_XBC, SSM_CONV_WIDTH - 1), pm(SSM_DIM),
                 pm(SSM_DIM), pm(SSM_BC), pm(SSM_BC), pm(SSM_DIM),
                 pl.BlockSpec((sb, LANES), lambda i: (i, 0)))
    return pl.pallas_call(
        _sample_pre_kernel,
        grid=(nb // sb,),
        in_specs=in_specs,
        out_specs=out_specs,
        out_shape=out_shape,
        compiler_params=pltpu.CompilerParams(
            dimension_semantics=("arbitrary",), vmem_limit_bytes=VMEM_LIMIT),
        name="sample_pre",
    )(x_pm, mod, mod, cst_t, sst_t, *consts)


def _sample_state_kernel(dec_ref, st_ref, c_ref, b_ref, xd_ref, st_out, yi_out):
    nt, sb = c_ref.shape[0], c_ref.shape[1]
    rows = nt * sb
    i = pl.program_id(0)
    cmat = c_ref[...].reshape(rows, SSM_BC).astype(BF16)
    bmat = b_ref[...].reshape(rows, SSM_BC)
    xd = xd_ref[...].reshape(rows, SSM_DIM)
    seq_of_row = lax.broadcasted_iota(jnp.int32, (rows, GROUP_DIM), 0) % sb
    seq_of_row_n = lax.broadcasted_iota(jnp.int32, (rows, SSM_STATE), 0) % sb
    accs = []
    for g in range(SSM_GROUPS):
        gs = slice(g * GROUP_DIM, (g + 1) * GROUP_DIM)
        ns = slice(g * SSM_STATE, (g + 1) * SSM_STATE)
        cg = cmat[:, ns]
        bg = bmat[:, ns]
        xd_t = xd[:, gs].T.astype(BF16)
        acc = jnp.zeros((rows, GROUP_DIM), F32)
        for b in range(sb):
            mine = seq_of_row == b
            h0 = st_ref[b, gs, :]
            acc = jnp.where(mine, _dot_nt(cg, h0.astype(BF16)), acc)
            b_mine = jnp.where(seq_of_row_n == b, bg, 0.0).astype(BF16)
            st = _dot(xd_t, b_mine)
            for r in range(HEADS_PER_GROUP):
                hs = slice(r * SSM_HEAD_DIM, (r + 1) * SSM_HEAD_DIM)
                dec = dec_ref[i * sb + b, g * HEADS_PER_GROUP + r]
                st_out[b, g * GROUP_DIM + r * SSM_HEAD_DIM:g * GROUP_DIM + (r + 1) * SSM_HEAD_DIM, :] = (
                    dec * h0[hs, :] + st[hs, :])
        accs.append(acc)
    yi_out[...] = jnp.concatenate(accs, axis=1).reshape(nt, sb, SSM_DIM)


def _sample_state_call(dec, state, c_pm, b_pm, xd_pm):
    nt, nb, _ = c_pm.shape
    sb = STATE_SEQ_BLOCK

    def pm(width):
        return pl.BlockSpec((nt, sb, width), lambda i: (0, i, 0))

    st_spec = pl.BlockSpec((sb, SSM_DIM, SSM_STATE), lambda i: (i, 0, 0))
    return pl.pallas_call(
        _sample_state_kernel,
        grid=(nb // sb,),
        in_specs=[pl.BlockSpec(memory_space=pltpu.SMEM), st_spec, pm(SSM_BC), pm(SSM_BC), pm(SSM_DIM)],
        out_specs=(st_spec, pm(SSM_DIM)),
        out_shape=(jax.ShapeDtypeStruct(state.shape, F32), jax.ShapeDtypeStruct((nt, nb, SSM_DIM), F32)),
        compiler_params=pltpu.CompilerParams(
            dimension_semantics=("arbitrary",), vmem_limit_bytes=VMEM_LIMIT),
        name="sample_state",
    )(dec, state, c_pm, b_pm, xd_pm)


def _sample_post_kernel(x_ref, gt_ref, h_ref, yp_ref, yi_ref, ec_ref, bc_ref, ng_ref, gpost_ref,
                        wzm_ref, wso_ref, wo_ref, y_out):
    reps = x_ref.shape[0] // gt_ref.shape[0]
    gate = jnp.concatenate([gt_ref[...]] * reps, axis=0)
    hb = h_ref[...]
    zm = [_dot(hb, wzm_ref[c]) for c in range(ZM_CHUNKS)]
    z = jnp.concatenate(zm[:ZM_CHUNKS // 2], axis=1)
    m = jnp.concatenate(zm[ZM_CHUNKS // 2:], axis=1)
    y = (yp_ref[...] + yi_ref[...] * ec_ref[...]) * _silu(z)
    branch_ssm = _dot(_group_rmsnorm(y, ng_ref[...]).astype(BF16), wso_ref[...])
    merged = _sig(m[:, 0:D_MODEL]) * bc_ref[...] + _sig(m[:, D_MODEL:]) * branch_ssm
    o = _dot(merged.astype(BF16), wo_ref[...])
    y_out[...] = _final(x_ref[...], gate, o, gpost_ref[...])


def _sample_post_call(x2, mod, h2, yp2, yi2, ec2, bc2, p):
    n = x2.shape[0]
    nb = mod.shape[0]
    rb = POST_ROWS
    consts = (p["ng"], p["g_post"], p["w_zm"], p["w_so"], p["w_o"])

    def rowblk(width):
        return pl.BlockSpec((rb, width), lambda i: (i, 0))

    in_specs = ([rowblk(D_MODEL), pl.BlockSpec((nb, D_MODEL), lambda i: (0, 2)), rowblk(D_MODEL),
                 rowblk(SSM_DIM), rowblk(SSM_DIM), rowblk(SSM_DIM), rowblk(D_MODEL)]
                + [_const(c.shape, 1) for c in consts])
    return pl.pallas_call(
        _sample_post_kernel,
        grid=(n // rb,),
        in_specs=in_specs,
        out_specs=rowblk(D_MODEL),
        out_shape=jax.ShapeDtypeStruct((n, D_MODEL), F32),
        compiler_params=pltpu.CompilerParams(
            dimension_semantics=("arbitrary",), vmem_limit_bytes=VMEM_LIMIT),
        name="sample_post",
    )(x2, mod, h2, yp2, yi2, ec2, bc2, *consts)


def _conf_roll_kernel(st_ref, u_ref, o_ref):
    keep = (CONV_WIDTH - 1) - u_ref.shape[1]
    o_ref[:, 0:keep, :] = st_ref[:, u_ref.shape[1]:, :]
    o_ref[:, keep:, :] = u_ref[...]


def _conf_roll_call(state, u_new):
    nb, hist, width = state.shape
    nt = u_new.shape[1]
    sb = STATE_SEQ_BLOCK
    return pl.pallas_call(
        _conf_roll_kernel,
        grid=(nb // sb,),
        in_specs=[pl.BlockSpec((sb, hist, width), lambda i: (i, 0, 0)),
                  pl.BlockSpec((sb, nt, width), lambda i: (i, 0, 0))],
        out_specs=pl.BlockSpec((sb, hist, width), lambda i: (i, 0, 0)),
        out_shape=jax.ShapeDtypeStruct(state.shape, F32),
        name="conf_roll",
    )(state, u_new)


def _selection_constants():
    tri = np.tril(np.ones((SSM_CHUNK, SSM_CHUNK), np.float32))
    tri3 = np.concatenate([tri, tri, tri], axis=1)
    e3 = np.zeros((LANES, SSM_DIM), np.float32)
    for piece in range(3):
        for hd in range(SSM_HEADS):
            e3[piece * SSM_HEADS + hd, hd * SSM_HEAD_DIM:(hd + 1) * SSM_HEAD_DIM] = 1.0
    return jnp.asarray(tri3, BF16), jnp.asarray(e3, BF16)


def _pad_lanes(v, n=LANES):
    return jnp.pad(v, ((0, 0), (0, n - v.shape[1])))


def kernel(x_prompt, x_sample, c_prompt, c_sample, state_conf_conv, state_ssm_conv, state_ssm, w_ada, b_ada, g_pre, g_post, w_in, conf_dw_w, conf_dw_b, conf_ln_g, conf_ln_b, w_conf_out, ssm_dw_w, ssm_dw_b, ssm_dt_bias, ssm_a_log, ssm_d, ssm_norm_g, w_ssm_out, w_o):
    bp = x_prompt.shape[0]
    nb, nt, _ = x_sample.shape
    l = 0
    mod = _ada(jnp.concatenate([c_prompt, c_sample], axis=0), w_ada[l], b_ada[l][None])

    wi = w_in[l]
    o_z = 3 * D_MODEL
    o_x = o_z + SSM_DIM
    o_dt = o_x + SSM_XBC
    o_m = o_dt + SSM_HEADS
    tri3, e3 = _selection_constants()
    def col_chunks(w):
        return w.reshape(w.shape[0], w.shape[1] // ZM_COLS, ZM_COLS).transpose(1, 0, 2)

    def lane_slabs(w, rows):
        w = jnp.pad(w, ((0, rows - w.shape[0]), (0, 0)))
        return w.reshape(rows, w.shape[1] // LANES, LANES).transpose(1, 0, 2)

    cw = jnp.pad(conf_dw_w[l], ((0, 1), (0, 0)))
    p = {
        "w_v": wi[:, :o_z].astype(BF16),
        "w_x": col_chunks(wi[:, o_x:o_dt].astype(BF16)),
        "w_zm": col_chunks(jnp.concatenate([wi[:, o_z:o_x], wi[:, o_m:]], axis=1).astype(BF16)),
        "w_dt": _pad_lanes(wi[:, o_dt:o_m]).astype(BF16),
        "w_co": w_conf_out[l].astype(BF16), "w_so": w_ssm_out[l].astype(BF16), "w_o": w_o[l].astype(BF16),
        "g_pre": g_pre[l][None], "g_post": g_post[l][None],
        "cw": cw, "cb": conf_dw_b[l][None],
        "cw_s": lane_slabs(cw, cw.shape[0]), "cb_s": lane_slabs(conf_dw_b[l][None], 1),
        "lng": conf_ln_g[l][None], "lnb": conf_ln_b[l][None],
        "sw": ssm_dw_w[l], "sb": ssm_dw_b[l][None],
        "sw_s": lane_slabs(ssm_dw_w[l], SUBLANES), "sb_s": lane_slabs(ssm_dw_b[l][None], 1),
        "dtb": _pad_lanes(ssm_dt_bias[l][None]), "alog": _pad_lanes(ssm_a_log[l][None]),
        "dexp": jnp.repeat(ssm_d[l], SSM_HEAD_DIM)[None], "ng": ssm_norm_g[l][None],
        "tri3": tri3, "e3": e3,
    }

    yp, conf_p, ssmc_p, st_p = _prompt_call(x_prompt, mod[:bp], p)
    st_p = st_p.reshape(1, bp, SSM_HEADS, SSM_HEAD_DIM, SSM_STATE)

    mod_s = mod[bp:]
    x_pm = x_sample.transpose(1, 0, 2)
    cst_t = state_conf_conv[l].transpose(1, 0, 2)
    sst_t = state_ssm_conv[l].transpose(1, 0, 2)
    h_pm, bc_pm, unew_pm, xnew_pm, yp_pm, ec_pm, c_pm, b_pm, xd_pm, dec = _sample_pre_call(
        x_pm, mod_s, cst_t, sst_t, p)
    st_s, yi_pm = _sample_state_call(dec[:, :SSM_HEADS], state_ssm[l].reshape(nb, SSM_DIM, SSM_STATE),
                                     c_pm, b_pm, xd_pm)

    def flat(v):
        return v.reshape(nt * nb, v.shape[-1])

    ys = _sample_post_call(flat(x_pm), mod_s, flat(h_pm), flat(yp_pm), flat(yi_pm), flat(ec_pm), flat(bc_pm), p)
    ys = ys.reshape(nt, nb, D_MODEL).transpose(1, 0, 2)
    conf_s = _conf_roll_call(state_conf_conv[l], unew_pm.transpose(1, 0, 2))
    ssmc_s = xnew_pm.transpose(1, 0, 2)
    st_s = st_s.reshape(1, nb, SSM_HEADS, SSM_HEAD_DIM, SSM_STATE)
    return (yp, ys, conf_p[None], ssmc_p[None], st_p, conf_s[None], ssmc_s[None], st_s)
```

```python
import functools

import jax
import jax.numpy as jnp
import numpy as np
from jax import lax
from jax.experimental import pallas as pl
from jax.experimental.pallas import tpu as pltpu

F32 = jnp.float32
BF16 = jnp.bfloat16

D_MODEL = 1024
CONV_WIDTH = 31
SSM_DIM = 2048
SSM_HEADS = 32
SSM_HEAD_DIM = 64
SSM_GROUPS = 8
SSM_STATE = 128
SSM_CONV_WIDTH = 4
SSM_BC = SSM_GROUPS * SSM_STATE
SSM_XBC = SSM_DIM + 2 * SSM_BC
SSM_CHUNK = 128
GROUP_DIM = SSM_DIM // SSM_GROUPS
HEADS_PER_GROUP = SSM_HEADS // SSM_GROUPS
EPS = 1e-6

SUBLANES = 8
LANES = 128
VMEM_LIMIT = 60 * 1024 * 1024

ZM_COLS = 512
ZM_CHUNKS = 2 * SSM_DIM // ZM_COLS

PROMPT_TILE = 256
CONF_HALO = 32
SSM_HALO = 8

SAMPLE_SEQ_BLOCK = 32
STATE_SEQ_BLOCK = 8
POST_ROWS = 256


def _sig(x):
    return 0.5 * jnp.tanh(0.5 * x) + 0.5


def _silu(x):
    return x * _sig(x)


def _pack_rows(w):
    k, n = w.shape[-2:]
    u = lax.bitcast_convert_type(w, jnp.uint16).astype(jnp.uint32).reshape(w.shape[:-2] + (k // 2, 2, n))
    return u[..., 0, :] | (u[..., 1, :] << 16)


def _w(ref, c0=None, c1=None):
    v = ref[...] if c0 is None else ref[:, c0:c1]
    return pltpu.bitcast(v, BF16)


def _dot(a, b):
    return jnp.dot(a, b, preferred_element_type=F32)


def _dot_nt(a, b):
    return lax.dot_general(a, b, (((1,), (1,)), ((), ())), preferred_element_type=F32)


def _dot_tn(a, b):
    return lax.dot_general(a, b, (((0,), (0,)), ((), ())), preferred_element_type=F32)


def _split3(v):
    hi = v.astype(BF16)
    r1 = v - hi.astype(F32)
    mid = r1.astype(BF16)
    lo = (r1 - mid.astype(F32)).astype(BF16)
    return hi, mid, lo


def _exact_left(sel3, v):
    hi, mid, lo = _split3(v)
    return _dot(sel3, jnp.concatenate([hi, mid, lo], axis=0))


def _expand_heads(v, e3):
    lane = lax.broadcasted_iota(jnp.int32, v.shape, 1)
    v = jnp.where(lane < SSM_HEADS, v, 0.0)
    hi = v.astype(BF16).astype(F32)
    r1 = v - hi
    mid = r1.astype(BF16).astype(F32)
    packed = hi + pltpu.roll(mid, SSM_HEADS, 1) + pltpu.roll(r1 - mid, 2 * SSM_HEADS, 1)
    return _dot(packed.astype(BF16), e3)


def _softplus(x):
    return jnp.maximum(x, 0.0) + jnp.log1p(jnp.exp(-jnp.abs(x)))


def _prenorm(x, g_pre, scale, shift):
    r = lax.rsqrt(jnp.mean(x * x, axis=-1, keepdims=True) + EPS)
    return ((x * r) * g_pre) * (1.0 + scale) + shift


def _layernorm(x, g, b):
    mu = jnp.mean(x, axis=-1, keepdims=True)
    xc = x - mu
    var = jnp.mean(xc * xc, axis=-1, keepdims=True)
    return (xc * lax.rsqrt(var + EPS)) * g + b


def _group_rmsnorm(y, g):
    outs = []
    for gi in range(SSM_GROUPS):
        yg = y[:, gi * GROUP_DIM:(gi + 1) * GROUP_DIM]
        r = lax.rsqrt(jnp.mean(yg * yg, axis=-1, keepdims=True) + EPS)
        outs.append((yg * r) * g[:, gi * GROUP_DIM:(gi + 1) * GROUP_DIM])
    return jnp.concatenate(outs, axis=1)


def _final(x, gate, o, g_post):
    r = lax.rsqrt(jnp.mean(o * o, axis=-1, keepdims=True) + EPS)
    return x + gate * ((o * r) * g_post)


CONV_ROW_STRIDE = 4
CONV_WINDOW = 128


def _to_slabs(buf, v, row0, slab0):
    for s in range(v.shape[1] // LANES):
        buf[slab0 + s, row0:row0 + v.shape[0], :] = v[:, s * LANES:(s + 1) * LANES]


def _conv_slab(buf, w_ref, b_ref, out, s, width, halo, rows, act):
    base = halo - (width - 1)
    n = CONV_WINDOW // CONV_ROW_STRIDE
    for w0 in range(0, rows, CONV_WINDOW):
        accs = [jnp.broadcast_to(b_ref[s], (n, LANES)) for _ in range(CONV_ROW_STRIDE)]
        for d in range(width + CONV_ROW_STRIDE - 1):
            v = buf[s, pl.ds(base + w0 + d, n, stride=CONV_ROW_STRIDE), :]
            for o in range(CONV_ROW_STRIDE):
                if 0 <= d - o < width:
                    accs[o] = accs[o] + jnp.broadcast_to(w_ref[s, d - o:d - o + 1, :], (n, LANES)) * v
        for o in range(CONV_ROW_STRIDE):
            out[s, pl.ds(w0 + o, n, stride=CONV_ROW_STRIDE), :] = act(accs[o])


def _carry_slabs(buf, tail_out, width, halo, rows):
    for s in range(buf.shape[0]):
        tail_out[:, s * LANES:(s + 1) * LANES] = buf[s, rows + halo - (width - 1):rows + halo, :]
        buf[s, 0:halo, :] = buf[s, rows:rows + halo, :]


def _ssd_intra_group(cg, bg, cum, cum_t, mask, xdt_g, g):
    rows = cg.shape[0]
    cb = _dot_nt(cg, bg)
    lane = lax.broadcasted_iota(jnp.int32, (rows, GROUP_DIM), 1)
    ms, xbd = [], []
    for r in range(HEADS_PER_GROUP):
        hd = g * HEADS_PER_GROUP + r
        seg = cum[:, hd:hd + 1] - cum_t[hd:hd + 1, :]
        dec = jnp.exp(jnp.where(mask, seg, -jnp.inf))
        ms.append((cb * dec).astype(BF16))
        in_head = (lane >= r * SSM_HEAD_DIM) & (lane < (r + 1) * SSM_HEAD_DIM)
        xbd.append(jnp.where(in_head, xdt_g, jnp.zeros_like(xdt_g)))
    return _dot(jnp.concatenate(ms, axis=1), jnp.concatenate(xbd, axis=0))


def _ada_kernel(c_ref, w_ref, b_ref, o_ref):
    s = _silu(c_ref[...]).astype(BF16)
    o_ref[...] = _dot(s, w_ref[...].astype(BF16)) + b_ref[...]


def _ada(c_all, w_ada, b_ada):
    n = c_all.shape[0]
    return pl.pallas_call(
        _ada_kernel,
        grid=(3,),
        in_specs=[
            pl.BlockSpec((n, D_MODEL), lambda k: (0, 0)),
            pl.BlockSpec((D_MODEL, D_MODEL), lambda k: (0, k)),
            pl.BlockSpec((1, D_MODEL), lambda k: (0, k)),
        ],
        out_specs=pl.BlockSpec((n, D_MODEL), lambda k: (0, k)),
        out_shape=jax.ShapeDtypeStruct((n, 3 * D_MODEL), F32),
        name="ada_mod",
    )(c_all, w_ada, b_ada)


def _prompt_kernel(x_ref, sh_ref, sc_ref, gt_ref, gpre_ref, gpost_ref, cw_ref, cb_ref, lng_ref, lnb_ref,
                   sw_ref, sb_ref, dtb_ref, alog_ref, dexp_ref, ng_ref, tri_ref, e3_ref,
                   wv_ref, wx_ref, wzm_ref, wdt_ref, wco_ref, wso_ref, wo_ref,
                   y_out, conf_out, ssmc_out, st_out,
                   h_ref, ubuf, cacc, xbuf, xact, ht_ref, zm_ref, yn_ref):
    T = PROMPT_TILE
    j = pl.program_id(1)

    @pl.when(j == 0)
    def _():
        ubuf[:, 0:CONF_HALO, :] = jnp.zeros((D_MODEL // LANES, CONF_HALO, LANES), F32)
        xbuf[:, 0:SSM_HALO, :] = jnp.zeros((SSM_XBC // LANES, SSM_HALO, LANES), F32)
        ht_ref[...] = jnp.zeros_like(ht_ref)

    h_ref[...] = _prenorm(x_ref[...], gpre_ref[...], sc_ref[...], sh_ref[...]).astype(BF16)

    u = _dot(h_ref[...], _w(wv_ref, 0, D_MODEL)) * _sig(_dot(h_ref[...], _w(wv_ref, D_MODEL, 2 * D_MODEL)))
    _to_slabs(ubuf, u, CONF_HALO, 0)
    c_slabs = D_MODEL // LANES
    spc = ZM_COLS // LANES
    assert c_slabs == ZM_CHUNKS == SSM_XBC // ZM_COLS

    def project_xbc(i):
        _to_slabs(xbuf, _dot(h_ref[...], _w(wx_ref.at[i])), SSM_HALO, i * spc)

    def conf_step(act):
        def body(i, carry):
            _conv_slab(ubuf, cw_ref, cb_ref, cacc, i, CONV_WIDTH, CONF_HALO, T, lambda v: v)
            zm_ref[i] = act(_dot(h_ref[...], _w(wzm_ref.at[i])))
            return carry
        return body

    lax.fori_loop(0, ZM_CHUNKS // 2, conf_step(_silu), 0)
    lax.fori_loop(ZM_CHUNKS // 2, ZM_CHUNKS, conf_step(_sig), 0)
    for i in range(ZM_CHUNKS):
        project_xbc(i)
        for q in range(spc):
            _conv_slab(xbuf, sw_ref, sb_ref, xact, i * spc + q, SSM_CONV_WIDTH, SSM_HALO, T, _silu)
    _carry_slabs(ubuf, conf_out, CONV_WIDTH, CONF_HALO, T)
    _carry_slabs(xbuf, ssmc_out, SSM_CONV_WIDTH, SSM_HALO, T)
    acc = jnp.concatenate([cacc[s] for s in range(c_slabs)], axis=1)
    uc = _silu(_layernorm(acc, lng_ref[...], lnb_ref[...])) * _silu(
        _dot(h_ref[...], _w(wv_ref, 2 * D_MODEL, 3 * D_MODEL)))
    branch_conv = _dot(uc.astype(BF16), _w(wco_ref))
    xs_slabs = SSM_DIM // LANES
    b_slab0 = xs_slabs
    c_slab0 = xs_slabs + SSM_GROUPS
    slabs_per_group = GROUP_DIM // LANES

    dt = _softplus(_dot(h_ref[...], _w(wdt_ref)) + dtb_ref[...])
    dta = dt * (-jnp.exp(alog_ref[...]))

    row = lax.broadcasted_iota(jnp.int32, (SSM_CHUNK, SSM_CHUNK), 0)
    col = lax.broadcasted_iota(jnp.int32, (SSM_CHUNK, SSM_CHUNK), 1)
    causal = col <= row

    for c in range(T // SSM_CHUNK):
        rs = slice(c * SSM_CHUNK, (c + 1) * SSM_CHUNK)
        cum = _exact_left(_w(tri_ref), dta[rs, :])
        cum_last = cum[SSM_CHUNK - 1:SSM_CHUNK, :]
        cum_t = cum.T
        fac = jnp.concatenate([
            dt[rs, :],
            jnp.exp(cum_last - cum),
            jnp.exp(cum),
            jnp.broadcast_to(jnp.exp(cum_last), (SUBLANES, LANES)),
        ], axis=0)
        facx = _expand_heads(fac, _w(e3_ref))
        dt_x = facx[0:SSM_CHUNK]
        dend_x = facx[SSM_CHUNK:2 * SSM_CHUNK]
        ecum_x = facx[2 * SSM_CHUNK:3 * SSM_CHUNK]
        cdec_x = facx[3 * SSM_CHUNK:3 * SSM_CHUNK + 1]
        xdt = jnp.concatenate([xact[s, rs, :] for s in range(xs_slabs)], axis=1) * dt_x
        xdt_b = xdt.astype(BF16)
        xdte_b = (xdt * dend_x).astype(BF16)
        for g in range(SSM_GROUPS):
            gs = slice(g * GROUP_DIM, (g + 1) * GROUP_DIM)
            ns = slice(g * SSM_STATE, (g + 1) * SSM_STATE)
            bg = xact[b_slab0 + g, rs, :].astype(BF16)
            cg = xact[c_slab0 + g, rs, :].astype(BF16)
            htg = ht_ref[:, gs]
            y_g = _dot(cg, htg.astype(BF16)) * ecum_x[:, gs]
            y_g = y_g + _ssd_intra_group(cg, bg, cum, cum_t, causal, xdt_b[:, gs], g)
            _to_slabs(xbuf, y_g, SSM_HALO + c * SSM_CHUNK, g * slabs_per_group)
            ht_ref[:, gs] = cdec_x[:, gs] * htg + _dot_tn(bg, xdte_b[:, gs])

    for g in range(SSM_GROUPS):
        gs = slice(g * GROUP_DIM, (g + 1) * GROUP_DIM)
        sl = range(g * slabs_per_group, (g + 1) * slabs_per_group)
        y_g = jnp.concatenate([xbuf[s, SSM_HALO:SSM_HALO + T, :] for s in sl], axis=1)
        xs_g = jnp.concatenate([xact[s] for s in sl], axis=1)
        zc = g * GROUP_DIM % ZM_COLS
        zs_g = zm_ref[g * GROUP_DIM // ZM_COLS, :, zc:zc + GROUP_DIM]
        y_g = (y_g + dexp_ref[:, gs] * xs_g) * zs_g
        r = lax.rsqrt(jnp.mean(y_g * y_g, axis=-1, keepdims=True) + EPS)
        yn_ref[:, gs] = ((y_g * r) * ng_ref[:, gs]).astype(BF16)
    branch_ssm = _dot(yn_ref[...], _w(wso_ref))

    m0 = SSM_DIM // ZM_COLS
    per = D_MODEL // ZM_COLS
    mg_conv = jnp.concatenate([zm_ref[m0 + q] for q in range(per)], axis=1)
    mg_ssm = jnp.concatenate([zm_ref[m0 + per + q] for q in range(per)], axis=1)
    merged = mg_conv * branch_conv + mg_ssm * branch_ssm
    o = _dot(merged.astype(BF16), _w(wo_ref))
    y_out[...] = _final(x_ref[...], gt_ref[...], o, gpost_ref[...])

    @pl.when(j == pl.num_programs(1) - 1)
    def _():
        st_out[...] = ht_ref[...].T


def _const(shape, ngrid):
    nd = len(shape)
    return pl.BlockSpec(shape, lambda *_: (0,) * nd, pipeline_mode=pl.Buffered(1))


def _prompt_call(x, mod, p):
    bsz, seq, _ = x.shape
    T = PROMPT_TILE
    nt = seq // T
    mod3 = mod.reshape(bsz, 1, 3 * D_MODEL)
    consts = (p["g_pre"], p["g_post"], p["cw_s"], p["cb_s"], p["lng"], p["lnb"], p["sw_s"], p["sb_s"], p["dtb"],
              p["alog"], p["dexp"], p["ng"], p["tri3"], p["e3"],
              p["w_v"], p["w_x"], p["w_zm"], p["w_dt"], p["w_co"], p["w_so"], p["w_o"])
    mod_specs = [pl.BlockSpec((None, 1, D_MODEL), functools.partial(lambda k, b, j: (b, 0, k), k))
                 for k in range(3)]
    in_specs = ([pl.BlockSpec((None, T, D_MODEL), lambda b, j: (b, j, 0))] + mod_specs
                + [_const(c.shape, 2) for c in consts])
    out_shape = (
        jax.ShapeDtypeStruct((bsz, seq, D_MODEL), F32),
        jax.ShapeDtypeStruct((bsz, CONV_WIDTH - 1, D_MODEL), F32),
        jax.ShapeDtypeStruct((bsz, SSM_CONV_WIDTH - 1, SSM_XBC), F32),
        jax.ShapeDtypeStruct((bsz, SSM_DIM, SSM_STATE), F32),
    )
    out_specs = (
        pl.BlockSpec((None, T, D_MODEL), lambda b, j: (b, j, 0)),
        pl.BlockSpec((None, CONV_WIDTH - 1, D_MODEL), lambda b, j: (b, 0, 0)),
        pl.BlockSpec((None, SSM_CONV_WIDTH - 1, SSM_XBC), lambda b, j: (b, 0, 0)),
        pl.BlockSpec((None, SSM_DIM, SSM_STATE), lambda b, j: (b, 0, 0)),
    )
    scratch = [
        pltpu.VMEM((T, D_MODEL), BF16),
        pltpu.VMEM((D_MODEL // LANES, T + CONF_HALO, LANES), F32),
        pltpu.VMEM((D_MODEL // LANES, T, LANES), F32),
        pltpu.VMEM((SSM_XBC // LANES, T + SSM_HALO, LANES), F32),
        pltpu.VMEM((SSM_XBC // LANES, T, LANES), F32),
        pltpu.VMEM((SSM_STATE, SSM_DIM), F32),
        pltpu.VMEM((ZM_CHUNKS, T, ZM_COLS), F32),
        pltpu.VMEM((T, SSM_DIM), BF16),
    ]
    return pl.pallas_call(
        _prompt_kernel,
        grid=(bsz, nt),
        in_specs=in_specs,
        out_specs=out_specs,
        out_shape=out_shape,
        scratch_shapes=scratch,
        compiler_params=pltpu.CompilerParams(
            dimension_semantics=("arbitrary", "arbitrary"), vmem_limit_bytes=VMEM_LIMIT),
        name="prompt_layer",
    )(x, mod3, mod3, mod3, *consts)


def _sample_pre_kernel(x_ref, sh_ref, sc_ref, cst_ref, sst_ref,
                       gpre_ref, cw_ref, cb_ref, lng_ref, lnb_ref, sw_ref, sb_ref, dtb_ref, alog_ref,
                       dexp_ref, e3_ref, wv_ref, wx_ref, wdt_ref, wco_ref,
                       h_out, bc_out, unew_out, xnew_out, yp_out, ec_out, c_out, b_out, xd_out, dec_out):
    nt, sb = x_ref.shape[0], x_ref.shape[1]
    rows = nt * sb

    def tile_rows(v):
        return jnp.concatenate([v] * nt, axis=0)

    x = x_ref[...].reshape(rows, D_MODEL)
    hb = _prenorm(x, gpre_ref[...], tile_rows(sc_ref[...]), tile_rows(sh_ref[...])).astype(BF16)
    h_out[...] = hb.reshape(nt, sb, D_MODEL)

    u = _dot(hb, _w(wv_ref, 0, D_MODEL)) * _sig(_dot(hb, _w(wv_ref, D_MODEL, 2 * D_MODEL)))
    unew_out[...] = u.reshape(nt, sb, D_MODEL)

    def conf_ext(jj):
        if jj < CONV_WIDTH - 1:
            return cst_ref[jj]
        return u[(jj - (CONV_WIDTH - 1)) * sb:(jj - (CONV_WIDTH - 2)) * sb, :]

    accs = []
    for t in range(nt):
        acc = jnp.broadcast_to(cb_ref[...], (sb, D_MODEL))
        for k in range(CONV_WIDTH):
            acc = acc + cw_ref[k:k + 1, :] * conf_ext(t + k)
        accs.append(acc)
    acc = jnp.concatenate(accs, axis=0)
    uc = _silu(_layernorm(acc, lng_ref[...], lnb_ref[...])) * _silu(_dot(hb, _w(wv_ref, 2 * D_MODEL, 3 * D_MODEL)))
    bc_out[...] = _dot(uc.astype(BF16), _w(wco_ref)).reshape(nt, sb, D_MODEL)

    xpre = jnp.concatenate([_dot(hb, _w(wx_ref.at[c])) for c in range(wx_ref.shape[0])], axis=1)
    xnew_out[...] = xpre[(nt - (SSM_CONV_WIDTH - 1)) * sb:, :].reshape(SSM_CONV_WIDTH - 1, sb, SSM_XBC)

    def ssm_ext(jj):
        if jj < SSM_CONV_WIDTH - 1:
            return sst_ref[jj]
        return xpre[(jj - (SSM_CONV_WIDTH - 1)) * sb:(jj - (SSM_CONV_WIDTH - 2)) * sb, :]

    xaccs = []
    for t in range(nt):
        xacc = jnp.broadcast_to(sb_ref[...], (sb, SSM_XBC))
        for k in range(SSM_CONV_WIDTH):
            xacc = xacc + sw_ref[k:k + 1, :] * ssm_ext(t + k)
        xaccs.append(xacc)
    xbc = _silu(jnp.concatenate(xaccs, axis=0))
    xs = xbc[:, 0:SSM_DIM]
    bmf = xbc[:, SSM_DIM:SSM_DIM + SSM_BC]
    cmf = xbc[:, SSM_DIM + SSM_BC:]
    b_out[...] = bmf.reshape(nt, sb, SSM_BC)
    c_out[...] = cmf.reshape(nt, sb, SSM_BC)
    bm = bmf.astype(BF16)
    cm = cmf.astype(BF16)

    dt = _softplus(_dot(hb, _w(wdt_ref)) + dtb_ref[...])
    dta = dt * (-jnp.exp(alog_ref[...]))
    cums = [dta[0:sb, :]]
    for t in range(1, nt):
        cums.append(cums[-1] + dta[t * sb:(t + 1) * sb, :])
    cum = jnp.concatenate(cums, axis=0)
    cum_last = tile_rows(cums[-1])
    dec_out[...] = jnp.exp(cums[-1])
    cum_t = cum.T

    fac = jnp.concatenate([dt, jnp.exp(cum_last - cum), jnp.exp(cum)], axis=0)
    facx = _expand_heads(fac, _w(e3_ref))
    dt_x = facx[0:rows]
    dend_x = facx[rows:2 * rows]
    ec_out[...] = facx[2 * rows:3 * rows].reshape(nt, sb, SSM_DIM)
    xdt = xs * dt_x
    xdt_b = xdt.astype(BF16)
    xd_out[...] = (xdt * dend_x).reshape(nt, sb, SSM_DIM)

    row = lax.broadcasted_iota(jnp.int32, (rows, rows), 0)
    col = lax.broadcasted_iota(jnp.int32, (rows, rows), 1)
    mask = (col <= row) & ((col % sb) == (row % sb))
    ys = []
    for g in range(SSM_GROUPS):
        gs = slice(g * GROUP_DIM, (g + 1) * GROUP_DIM)
        ns = slice(g * SSM_STATE, (g + 1) * SSM_STATE)
        ys.append(_ssd_intra_group(cm[:, ns], bm[:, ns], cum, cum_t, mask, xdt_b[:, gs], g))
    yp = jnp.concatenate(ys, axis=1) + dexp_ref[...] * xs
    yp_out[...] = yp.reshape(nt, sb, SSM_DIM)


def _sample_pre_call(x_pm, mod, cst_t, sst_t, p):
    nt, nb, _ = x_pm.shape
    sb = SAMPLE_SEQ_BLOCK
    consts = (p["g_pre"], p["cw"], p["cb"], p["lng"], p["lnb"], p["sw"], p["sb"], p["dtb"], p["alog"],
              p["dexp"], p["e3"], p["w_v"], p["w_x"], p["w_dt"], p["w_co"])

    def pm(width, n=nt):
        return pl.BlockSpec((n, sb, width), lambda i: (0, i, 0))

    in_specs = ([pm(D_MODEL),
                 pl.BlockSpec((sb, D_MODEL), lambda i: (i, 0)),
                 pl.BlockSpec((sb, D_MODEL), lambda i: (i, 1)),
                 pm(D_MODEL, CONV_WIDTH - 1), pm(SSM_XBC, SSM_CONV_WIDTH - 1)]
                + [_const(c.shape, 1) for c in consts])

    def sds(n, width, dtype=F32):
        return jax.ShapeDtypeStruct((n, nb, width), dtype)

    out_shape = (sds(nt, D_MODEL, BF16), sds(nt, D_MODEL), sds(nt, D_MODEL),
                 sds(SSM_CONV_WIDTH - 1, SSM_XBC), sds(nt, SSM_DIM), sds(nt, SSM_DIM),
                 sds(nt, SSM_BC), sds(nt, SSM_BC), sds(nt, SSM_DIM),
                 jax.ShapeDtypeStruct((nb, LANES), F32))
    out_specs = (pm(D_MODEL), pm(D_MODEL), pm(D_MODEL), pm(SSM_XBC, SSM_CONV_WIDTH - 1), pm(SSM_DIM),
                 pm(SSM_DIM), pm(SSM_BC), pm(SSM_BC), pm(SSM_DIM),
                 pl.BlockSpec((sb, LANES), lambda i: (i, 0)))
    return pl.pallas_call(
        _sample_pre_kernel,
        grid=(nb // sb,),
        in_specs=in_specs,
        out_specs=out_specs,
        out_shape=out_shape,
        compiler_params=pltpu.CompilerParams(
            dimension_semantics=("arbitrary",), vmem_limit_bytes=VMEM_LIMIT),
        name="sample_pre",
    )(x_pm, mod, mod, cst_t, sst_t, *consts)


def _sample_state_kernel(dec_ref, st_ref, c_ref, b_ref, xd_ref, st_out, yi_out):
    nt, sb = c_ref.shape[0], c_ref.shape[1]
    rows = nt * sb
    i = pl.program_id(0)
    cmat = c_ref[...].reshape(rows, SSM_BC).astype(BF16)
    bmat = b_ref[...].reshape(rows, SSM_BC)
    xd = xd_ref[...].reshape(rows, SSM_DIM)
    seq_of_row = lax.broadcasted_iota(jnp.int32, (rows, GROUP_DIM), 0) % sb
    seq_of_row_n = lax.broadcasted_iota(jnp.int32, (rows, SSM_STATE), 0) % sb
    accs = []
    for g in range(SSM_GROUPS):
        gs = slice(g * GROUP_DIM, (g + 1) * GROUP_DIM)
        ns = slice(g * SSM_STATE, (g + 1) * SSM_STATE)
        cg = cmat[:, ns]
        bg = bmat[:, ns]
        xd_t = xd[:, gs].T.astype(BF16)
        acc = jnp.zeros((rows, GROUP_DIM), F32)
        for b in range(sb):
            mine = seq_of_row == b
            h0 = st_ref[b, gs, :]
            acc = jnp.where(mine, _dot_nt(cg, h0.astype(BF16)), acc)
            b_mine = jnp.where(seq_of_row_n == b, bg, 0.0).astype(BF16)
            st = _dot(xd_t, b_mine)
            for r in range(HEADS_PER_GROUP):
                hs = slice(r * SSM_HEAD_DIM, (r + 1) * SSM_HEAD_DIM)
                dec = dec_ref[i * sb + b, g * HEADS_PER_GROUP + r]
                st_out[b, g * GROUP_DIM + r * SSM_HEAD_DIM:g * GROUP_DIM + (r + 1) * SSM_HEAD_DIM, :] = (
                    dec * h0[hs, :] + st[hs, :])
        accs.append(acc)
    yi_out[...] = jnp.concatenate(accs, axis=1).reshape(nt, sb, SSM_DIM)


def _sample_state_call(dec, state, c_pm, b_pm, xd_pm):
    nt, nb, _ = c_pm.shape
    sb = STATE_SEQ_BLOCK

    def pm(width):
        return pl.BlockSpec((nt, sb, width), lambda i: (0, i, 0))

    st_spec = pl.BlockSpec((sb, SSM_DIM, SSM_STATE), lambda i: (i, 0, 0))
    return pl.pallas_call(
        _sample_state_kernel,
        grid=(nb // sb,),
        in_specs=[pl.BlockSpec(memory_space=pltpu.SMEM), st_spec, pm(SSM_BC), pm(SSM_BC), pm(SSM_DIM)],
        out_specs=(st_spec, pm(SSM_DIM)),
        out_shape=(jax.ShapeDtypeStruct(state.shape, F32), jax.ShapeDtypeStruct((nt, nb, SSM_DIM), F32)),
        compiler_params=pltpu.CompilerParams(
            dimension_semantics=("arbitrary",), vmem_limit_bytes=VMEM_LIMIT),
        name="sample_state",
    )(dec, state, c_pm, b_pm, xd_pm)


def _sample_post_kernel(x_ref, gt_ref, h_ref, yp_ref, yi_ref, ec_ref, bc_ref, ng_ref, gpost_ref,
                        wzm_ref, wso_ref, wo_ref, y_out):
    reps = x_ref.shape[0] // gt_ref.shape[0]
    gate = jnp.concatenate([gt_ref[...]] * reps, axis=0)
    hb = h_ref[...]
    zm = [_dot(hb, _w(wzm_ref.at[c])) for c in range(ZM_CHUNKS)]
    z = jnp.concatenate(zm[:ZM_CHUNKS // 2], axis=1)
    m = jnp.concatenate(zm[ZM_CHUNKS // 2:], axis=1)
    y = (yp_ref[...] + yi_ref[...] * ec_ref[...]) * _silu(z)
    branch_ssm = _dot(_group_rmsnorm(y, ng_ref[...]).astype(BF16), _w(wso_ref))
    merged = _sig(m[:, 0:D_MODEL]) * bc_ref[...] + _sig(m[:, D_MODEL:]) * branch_ssm
    o = _dot(merged.astype(BF16), _w(wo_ref))
    y_out[...] = _final(x_ref[...], gate, o, gpost_ref[...])


def _sample_post_call(x2, mod, h2, yp2, yi2, ec2, bc2, p):
    n = x2.shape[0]
    nb = mod.shape[0]
    rb = POST_ROWS
    consts = (p["ng"], p["g_post"], p["w_zm"], p["w_so"], p["w_o"])

    def rowblk(width):
        return pl.BlockSpec((rb, width), lambda i: (i, 0))

    in_specs = ([rowblk(D_MODEL), pl.BlockSpec((nb, D_MODEL), lambda i: (0, 2)), rowblk(D_MODEL),
                 rowblk(SSM_DIM), rowblk(SSM_DIM), rowblk(SSM_DIM), rowblk(D_MODEL)]
                + [_const(c.shape, 1) for c in consts])
    return pl.pallas_call(
        _sample_post_kernel,
        grid=(n // rb,),
        in_specs=in_specs,
        out_specs=rowblk(D_MODEL),
        out_shape=jax.ShapeDtypeStruct((n, D_MODEL), F32),
        compiler_params=pltpu.CompilerParams(
            dimension_semantics=("arbitrary",), vmem_limit_bytes=VMEM_LIMIT),
        name="sample_post",
    )(x2, mod, h2, yp2, yi2, ec2, bc2, *consts)


def _conf_roll_kernel(st_ref, u_ref, o_ref):
    keep = (CONV_WIDTH - 1) - u_ref.shape[1]
    o_ref[:, 0:keep, :] = st_ref[:, u_ref.shape[1]:, :]
    o_ref[:, keep:, :] = u_ref[...]


def _conf_roll_call(state, u_new):
    nb, hist, width = state.shape
    nt = u_new.shape[1]
    sb = STATE_SEQ_BLOCK
    return pl.pallas_call(
        _conf_roll_kernel,
        grid=(nb // sb,),
        in_specs=[pl.BlockSpec((sb, hist, width), lambda i: (i, 0, 0)),
                  pl.BlockSpec((sb, nt, width), lambda i: (i, 0, 0))],
        out_specs=pl.BlockSpec((sb, hist, width), lambda i: (i, 0, 0)),
        out_shape=jax.ShapeDtypeStruct(state.shape, F32),
        name="conf_roll",
    )(state, u_new)


def _selection_constants():
    tri = np.tril(np.ones((SSM_CHUNK, SSM_CHUNK), np.float32))
    tri3 = np.concatenate([tri, tri, tri], axis=1)
    e3 = np.zeros((LANES, SSM_DIM), np.float32)
    for piece in range(3):
        for hd in range(SSM_HEADS):
            e3[piece * SSM_HEADS + hd, hd * SSM_HEAD_DIM:(hd + 1) * SSM_HEAD_DIM] = 1.0
    return jnp.asarray(tri3, BF16), jnp.asarray(e3, BF16)


def _pad_lanes(v, n=LANES):
    return jnp.pad(v, ((0, 0), (0, n - v.shape[1])))


def kernel(x_prompt, x_sample, c_prompt, c_sample, state_conf_conv, state_ssm_conv, state_ssm, w_ada, b_ada, g_pre, g_post, w_in, conf_dw_w, conf_dw_b, conf_ln_g, conf_ln_b, w_conf_out, ssm_dw_w, ssm_dw_b, ssm_dt_bias, ssm_a_log, ssm_d, ssm_norm_g, w_ssm_out, w_o):
    bp = x_prompt.shape[0]
    nb, nt, _ = x_sample.shape
    l = 0
    mod = _ada(jnp.concatenate([c_prompt, c_sample], axis=0), w_ada[l], b_ada[l][None])

    wi = w_in[l]
    o_z = 3 * D_MODEL
    o_x = o_z + SSM_DIM
    o_dt = o_x + SSM_XBC
    o_m = o_dt + SSM_HEADS
    tri3, e3 = _selection_constants()
    def col_chunks(w):
        return w.reshape(w.shape[0], w.shape[1] // ZM_COLS, ZM_COLS).transpose(1, 0, 2)

    def lane_slabs(w, rows):
        w = jnp.pad(w, ((0, rows - w.shape[0]), (0, 0)))
        return w.reshape(rows, w.shape[1] // LANES, LANES).transpose(1, 0, 2)

    cw = jnp.pad(conf_dw_w[l], ((0, 1), (0, 0)))
    p = {
        "w_v": wi[:, :o_z].astype(BF16),
        "w_x": col_chunks(wi[:, o_x:o_dt].astype(BF16)),
        "w_zm": col_chunks(jnp.concatenate([wi[:, o_z:o_x], wi[:, o_m:]], axis=1).astype(BF16)),
        "w_dt": _pad_lanes(wi[:, o_dt:o_m]).astype(BF16),
        "w_co": w_conf_out[l].astype(BF16), "w_so": w_ssm_out[l].astype(BF16), "w_o": w_o[l].astype(BF16),
        "g_pre": g_pre[l][None], "g_post": g_post[l][None],
        "cw": cw, "cb": conf_dw_b[l][None],
        "cw_s": lane_slabs(cw, cw.shape[0]), "cb_s": lane_slabs(conf_dw_b[l][None], 1),
        "lng": conf_ln_g[l][None], "lnb": conf_ln_b[l][None],
        "sw": ssm_dw_w[l], "sb": ssm_dw_b[l][None],
        "sw_s": lane_slabs(ssm_dw_w[l], SUBLANES), "sb_s": lane_slabs(ssm_dw_b[l][None], 1),
        "dtb": _pad_lanes(ssm_dt_bias[l][None]), "alog": _pad_lanes(ssm_a_log[l][None]),
        "dexp": jnp.repeat(ssm_d[l], SSM_HEAD_DIM)[None], "ng": ssm_norm_g[l][None],
        "tri3": tri3, "e3": e3,
    }
    for name in ("w_v", "w_x", "w_zm", "w_dt", "w_co", "w_so", "w_o", "tri3", "e3"):
        p[name] = _pack_rows(p[name])

    yp, conf_p, ssmc_p, st_p = _prompt_call(x_prompt, mod[:bp], p)
    st_p = st_p.reshape(1, bp, SSM_HEADS, SSM_HEAD_DIM, SSM_STATE)

    mod_s = mod[bp:]
    x_pm = x_sample.transpose(1, 0, 2)
    cst_t = state_conf_conv[l].transpose(1, 0, 2)
    sst_t = state_ssm_conv[l].transpose(1, 0, 2)
    h_pm, bc_pm, unew_pm, xnew_pm, yp_pm, ec_pm, c_pm, b_pm, xd_pm, dec = _sample_pre_call(
        x_pm, mod_s, cst_t, sst_t, p)
    st_s, yi_pm = _sample_state_call(dec[:, :SSM_HEADS], state_ssm[l].reshape(nb, SSM_DIM, SSM_STATE),
                                     c_pm, b_pm, xd_pm)

    def flat(v):
        return v.reshape(nt * nb, v.shape[-1])

    ys = _sample_post_call(flat(x_pm), mod_s, flat(h_pm), flat(yp_pm), flat(yi_pm), flat(ec_pm), flat(bc_pm), p)
    ys = ys.reshape(nt, nb, D_MODEL).transpose(1, 0, 2)
    conf_s = _conf_roll_call(state_conf_conv[l], unew_pm.transpose(1, 0, 2))
    ssmc_s = xnew_pm.transpose(1, 0, 2)
    st_s = st_s.reshape(1, nb, SSM_HEADS, SSM_HEAD_DIM, SSM_STATE)
    return (yp, ys, conf_p[None], ssmc_p[None], st_p, conf_s[None], ssmc_s[None], st_s)
```

```python
import functools

import jax
import jax.numpy as jnp
import numpy as np
from jax import lax
from jax.experimental import pallas as pl
from jax.experimental.pallas import tpu as pltpu

F32 = jnp.float32
BF16 = jnp.bfloat16

D_MODEL = 1024
CONV_WIDTH = 31
SSM_DIM = 2048
SSM_HEADS = 32
SSM_HEAD_DIM = 64
SSM_GROUPS = 8
SSM_STATE = 128
SSM_CONV_WIDTH = 4
SSM_BC = SSM_GROUPS * SSM_STATE
SSM_XBC = SSM_DIM + 2 * SSM_BC
SSM_CHUNK = 128
GROUP_DIM = SSM_DIM // SSM_GROUPS
HEADS_PER_GROUP = SSM_HEADS // SSM_GROUPS
EPS = 1e-6

SUBLANES = 8
LANES = 128
VMEM_LIMIT = 60 * 1024 * 1024

ZM_COLS = 512
ZM_CHUNKS = 2 * SSM_DIM // ZM_COLS

PROMPT_TILE = 256
CONF_HALO = 32
SSM_HALO = 8

CONV_STATE_SEQ_BLOCK = 16
SAMPLE_SEQ_BLOCK = 32
STATE_SEQ_BLOCK = 8
POST_ROWS = 256


def _sig(x):
    return 0.5 * jnp.tanh(0.5 * x) + 0.5


def _silu(x):
    return x * _sig(x)


def _w(ref, c0=None, c1=None):
    return ref[...] if c0 is None else ref[:, c0:c1]


def _chunk(ref, i):
    return ref[:, i * ZM_COLS:(i + 1) * ZM_COLS]


def _dot(a, b):
    return jnp.dot(a, b, preferred_element_type=F32)


def _dot_nt(a, b):
    return lax.dot_general(a, b, (((1,), (1,)), ((), ())), preferred_element_type=F32)


def _dot_tn(a, b):
    return lax.dot_general(a, b, (((0,), (0,)), ((), ())), preferred_element_type=F32)


def _split3(v):
    hi = v.astype(BF16)
    r1 = v - hi.astype(F32)
    mid = r1.astype(BF16)
    lo = (r1 - mid.astype(F32)).astype(BF16)
    return hi, mid, lo


def _exact_left(sel3, v):
    hi, mid, lo = _split3(v)
    return _dot(sel3, jnp.concatenate([hi, mid, lo], axis=0))


def _expand_heads(v, e3):
    lane = lax.broadcasted_iota(jnp.int32, v.shape, 1)
    v = jnp.where(lane < SSM_HEADS, v, 0.0)
    hi = v.astype(BF16).astype(F32)
    r1 = v - hi
    mid = r1.astype(BF16).astype(F32)
    packed = hi + pltpu.roll(mid, SSM_HEADS, 1) + pltpu.roll(r1 - mid, 2 * SSM_HEADS, 1)
    return _dot(packed.astype(BF16), e3)


def _softplus(x):
    return jnp.maximum(x, 0.0) + jnp.log1p(jnp.exp(-jnp.abs(x)))


def _prenorm(x, g_pre, scale, shift):
    r = lax.rsqrt(jnp.mean(x * x, axis=-1, keepdims=True) + EPS)
    return ((x * r) * g_pre) * (1.0 + scale) + shift


def _layernorm(x, g, b):
    mu = jnp.mean(x, axis=-1, keepdims=True)
    xc = x - mu
    var = jnp.mean(xc * xc, axis=-1, keepdims=True)
    return (xc * lax.rsqrt(var + EPS)) * g + b


def _group_rmsnorm(y, g):
    outs = []
    for gi in range(SSM_GROUPS):
        yg = y[:, gi * GROUP_DIM:(gi + 1) * GROUP_DIM]
        r = lax.rsqrt(jnp.mean(yg * yg, axis=-1, keepdims=True) + EPS)
        outs.append((yg * r) * g[:, gi * GROUP_DIM:(gi + 1) * GROUP_DIM])
    return jnp.concatenate(outs, axis=1)


def _final(x, gate, o, g_post):
    r = lax.rsqrt(jnp.mean(o * o, axis=-1, keepdims=True) + EPS)
    return x + gate * ((o * r) * g_post)


CONV_ROW_STRIDE = 4
CONV_WINDOW = 128


def _to_slabs(buf, v, row0, slab0):
    for s in range(v.shape[1] // LANES):
        buf[slab0 + s, row0:row0 + v.shape[0], :] = v[:, s * LANES:(s + 1) * LANES]


def _conv_slab(buf, w_ref, b_ref, out, s, width, halo, rows, act):
    base = halo - (width - 1)
    n = CONV_WINDOW // CONV_ROW_STRIDE
    for w0 in range(0, rows, CONV_WINDOW):
        accs = [jnp.broadcast_to(b_ref[s], (n, LANES)) for _ in range(CONV_ROW_STRIDE)]
        for d in range(width + CONV_ROW_STRIDE - 1):
            v = buf[s, pl.ds(base + w0 + d, n, stride=CONV_ROW_STRIDE), :]
            for o in range(CONV_ROW_STRIDE):
                if 0 <= d - o < width:
                    accs[o] = accs[o] + jnp.broadcast_to(w_ref[s, d - o:d - o + 1, :], (n, LANES)) * v
        for o in range(CONV_ROW_STRIDE):
            out[s, pl.ds(w0 + o, n, stride=CONV_ROW_STRIDE), :] = act(accs[o])


def _carry_slabs(buf, tail_out, width, halo, rows):
    for s in range(buf.shape[0]):
        tail_out[:, s * LANES:(s + 1) * LANES] = buf[s, rows + halo - (width - 1):rows + halo, :]
        buf[s, 0:halo, :] = buf[s, rows:rows + halo, :]


def _ssd_intra_group(cg, bg, cum, cum_t, mask, xdt_g, g):
    rows = cg.shape[0]
    cb = _dot_nt(cg, bg)
    lane = lax.broadcasted_iota(jnp.int32, (rows, GROUP_DIM), 1)
    ms, xbd = [], []
    for r in range(HEADS_PER_GROUP):
        hd = g * HEADS_PER_GROUP + r
        seg = cum[:, hd:hd + 1] - cum_t[hd:hd + 1, :]
        dec = jnp.exp(jnp.where(mask, seg, -jnp.inf))
        ms.append((cb * dec).astype(BF16))
        in_head = (lane >= r * SSM_HEAD_DIM) & (lane < (r + 1) * SSM_HEAD_DIM)
        xbd.append(jnp.where(in_head, xdt_g, jnp.zeros_like(xdt_g)))
    return _dot(jnp.concatenate(ms, axis=1), jnp.concatenate(xbd, axis=0))


def _ada_kernel(c_ref, w_ref, b_ref, o_ref):
    s = _silu(c_ref[...]).astype(BF16)
    o_ref[...] = _dot(s, w_ref[...].astype(BF16)) + b_ref[...]


def _ada(c_all, w_ada, b_ada):
    n = c_all.shape[0]
    return pl.pallas_call(
        _ada_kernel,
        grid=(3,),
        in_specs=[
            pl.BlockSpec((n, D_MODEL), lambda k: (0, 0)),
            pl.BlockSpec((D_MODEL, D_MODEL), lambda k: (0, k)),
            pl.BlockSpec((1, D_MODEL), lambda k: (0, k)),
        ],
        out_specs=pl.BlockSpec((n, D_MODEL), lambda k: (0, k)),
        out_shape=jax.ShapeDtypeStruct((n, 3 * D_MODEL), F32),
        name="ada_mod",
    )(c_all, w_ada, b_ada)


def _prompt_kernel(x_ref, sh_ref, sc_ref, gt_ref, gpre_ref, gpost_ref, cw_ref, cb_ref, lng_ref, lnb_ref,
                   sw_ref, sb_ref, dtb_ref, alog_ref, dexp_ref, ng_ref, tri_ref, e3_ref,
                   wv_ref, wx_ref, wzm_ref, wdt_ref, wco_ref, wso_ref, wo_ref,
                   y_out, conf_out, ssmc_out, st_out,
                   h_ref, ubuf, cacc, xbuf, xact, ht_ref, zm_ref, yn_ref):
    T = PROMPT_TILE
    j = pl.program_id(1)

    @pl.when(j == 0)
    def _():
        ubuf[:, 0:CONF_HALO, :] = jnp.zeros((D_MODEL // LANES, CONF_HALO, LANES), F32)
        xbuf[:, 0:SSM_HALO, :] = jnp.zeros((SSM_XBC // LANES, SSM_HALO, LANES), F32)
        ht_ref[...] = jnp.zeros_like(ht_ref)

    h_ref[...] = _prenorm(x_ref[...], gpre_ref[...], sc_ref[...], sh_ref[...]).astype(BF16)

    u = _dot(h_ref[...], _w(wv_ref, 0, D_MODEL)) * _sig(_dot(h_ref[...], _w(wv_ref, D_MODEL, 2 * D_MODEL)))
    _to_slabs(ubuf, u, CONF_HALO, 0)
    c_slabs = D_MODEL // LANES
    spc = ZM_COLS // LANES
    assert c_slabs == ZM_CHUNKS == SSM_XBC // ZM_COLS

    def project_xbc(i):
        _to_slabs(xbuf, _dot(h_ref[...], _chunk(wx_ref, i)), SSM_HALO, i * spc)

    project_xbc(0)
    for i in range(ZM_CHUNKS):
        _conv_slab(ubuf, cw_ref, cb_ref, cacc, i, CONV_WIDTH, CONF_HALO, T, lambda v: v)
        for q in range(spc):
            _conv_slab(xbuf, sw_ref, sb_ref, xact, i * spc + q, SSM_CONV_WIDTH, SSM_HALO, T, _silu)
        if i + 1 < ZM_CHUNKS:
            project_xbc(i + 1)
        zm_ref[i] = (_silu if i < ZM_CHUNKS // 2 else _sig)(_dot(h_ref[...], _chunk(wzm_ref, i)))
    _carry_slabs(ubuf, conf_out, CONV_WIDTH, CONF_HALO, T)
    _carry_slabs(xbuf, ssmc_out, SSM_CONV_WIDTH, SSM_HALO, T)
    acc = jnp.concatenate([cacc[s] for s in range(c_slabs)], axis=1)
    uc = _silu(_layernorm(acc, lng_ref[...], lnb_ref[...])) * _silu(
        _dot(h_ref[...], _w(wv_ref, 2 * D_MODEL, 3 * D_MODEL)))
    branch_conv = _dot(uc.astype(BF16), _w(wco_ref))
    xs_slabs = SSM_DIM // LANES
    b_slab0 = xs_slabs
    c_slab0 = xs_slabs + SSM_GROUPS
    slabs_per_group = GROUP_DIM // LANES

    dt = _softplus(_dot(h_ref[...], _w(wdt_ref)) + dtb_ref[...])
    dta = dt * (-jnp.exp(alog_ref[...]))

    row = lax.broadcasted_iota(jnp.int32, (SSM_CHUNK, SSM_CHUNK), 0)
    col = lax.broadcasted_iota(jnp.int32, (SSM_CHUNK, SSM_CHUNK), 1)
    causal = col <= row

    for c in range(T // SSM_CHUNK):
        rs = slice(c * SSM_CHUNK, (c + 1) * SSM_CHUNK)
        cum = _exact_left(_w(tri_ref), dta[rs, :])
        cum_last = cum[SSM_CHUNK - 1:SSM_CHUNK, :]
        cum_t = cum.T
        fac = jnp.concatenate([
            dt[rs, :],
            jnp.exp(cum_last - cum),
            jnp.exp(cum),
            jnp.broadcast_to(jnp.exp(cum_last), (SUBLANES, LANES)),
        ], axis=0)
        facx = _expand_heads(fac, _w(e3_ref))
        dt_x = facx[0:SSM_CHUNK]
        dend_x = facx[SSM_CHUNK:2 * SSM_CHUNK]
        ecum_x = facx[2 * SSM_CHUNK:3 * SSM_CHUNK]
        cdec_x = facx[3 * SSM_CHUNK:3 * SSM_CHUNK + 1]
        xdt = jnp.concatenate([xact[s, rs, :] for s in range(xs_slabs)], axis=1) * dt_x
        xdt_b = xdt.astype(BF16)
        xdte_b = (xdt * dend_x).astype(BF16)
        for g in range(SSM_GROUPS):
            gs = slice(g * GROUP_DIM, (g + 1) * GROUP_DIM)
            ns = slice(g * SSM_STATE, (g + 1) * SSM_STATE)
            bg = xact[b_slab0 + g, rs, :].astype(BF16)
            cg = xact[c_slab0 + g, rs, :].astype(BF16)
            htg = ht_ref[:, gs]
            y_g = _dot(cg, htg.astype(BF16)) * ecum_x[:, gs]
            y_g = y_g + _ssd_intra_group(cg, bg, cum, cum_t, causal, xdt_b[:, gs], g)
            _to_slabs(xbuf, y_g, SSM_HALO + c * SSM_CHUNK, g * slabs_per_group)
            ht_ref[:, gs] = cdec_x[:, gs] * htg + _dot_tn(bg, xdte_b[:, gs])

    for g in range(SSM_GROUPS):
        gs = slice(g * GROUP_DIM, (g + 1) * GROUP_DIM)
        sl = range(g * slabs_per_group, (g + 1) * slabs_per_group)
        y_g = jnp.concatenate([xbuf[s, SSM_HALO:SSM_HALO + T, :] for s in sl], axis=1)
        xs_g = jnp.concatenate([xact[s] for s in sl], axis=1)
        zc = g * GROUP_DIM % ZM_COLS
        zs_g = zm_ref[g * GROUP_DIM // ZM_COLS, :, zc:zc + GROUP_DIM]
        y_g = (y_g + dexp_ref[:, gs] * xs_g) * zs_g
        r = lax.rsqrt(jnp.mean(y_g * y_g, axis=-1, keepdims=True) + EPS)
        yn_ref[:, gs] = ((y_g * r) * ng_ref[:, gs]).astype(BF16)
    branch_ssm = _dot(yn_ref[...], _w(wso_ref))

    m0 = SSM_DIM // ZM_COLS
    per = D_MODEL // ZM_COLS
    mg_conv = jnp.concatenate([zm_ref[m0 + q] for q in range(per)], axis=1)
    mg_ssm = jnp.concatenate([zm_ref[m0 + per + q] for q in range(per)], axis=1)
    merged = mg_conv * branch_conv + mg_ssm * branch_ssm
    o = _dot(merged.astype(BF16), _w(wo_ref))
    y_out[...] = _final(x_ref[...], gt_ref[...], o, gpost_ref[...])

    @pl.when(j == pl.num_programs(1) - 1)
    def _():
        st_out[...] = ht_ref[...].T


def _const(shape, ngrid):
    nd = len(shape)
    return pl.BlockSpec(shape, lambda *_: (0,) * nd, pipeline_mode=pl.Buffered(1))


def _prompt_call(x, mod, p):
    bsz, seq, _ = x.shape
    T = PROMPT_TILE
    nt = seq // T
    mod3 = mod.reshape(bsz, 1, 3 * D_MODEL)
    consts = (p["g_pre"], p["g_post"], p["cw_s"], p["cb_s"], p["lng"], p["lnb"], p["sw_s"], p["sb_s"], p["dtb"],
              p["alog"], p["dexp"], p["ng"], p["tri3"], p["e3"],
              p["w_v"], p["w_x"], p["w_zm"], p["w_dt"], p["w_co"], p["w_so"], p["w_o"])
    mod_specs = [pl.BlockSpec((None, 1, D_MODEL), functools.partial(lambda k, b, j: (b, 0, k), k))
                 for k in range(3)]
    in_specs = ([pl.BlockSpec((None, T, D_MODEL), lambda b, j: (b, j, 0))] + mod_specs
                + [_const(c.shape, 2) for c in consts])
    out_shape = (
        jax.ShapeDtypeStruct((bsz, seq, D_MODEL), F32),
        jax.ShapeDtypeStruct((bsz, CONV_WIDTH - 1, D_MODEL), F32),
        jax.ShapeDtypeStruct((bsz, SSM_CONV_WIDTH - 1, SSM_XBC), F32),
        jax.ShapeDtypeStruct((bsz, SSM_DIM, SSM_STATE), F32),
    )
    out_specs = (
        pl.BlockSpec((None, T, D_MODEL), lambda b, j: (b, j, 0)),
        pl.BlockSpec((None, CONV_WIDTH - 1, D_MODEL), lambda b, j: (b, 0, 0)),
        pl.BlockSpec((None, SSM_CONV_WIDTH - 1, SSM_XBC), lambda b, j: (b, 0, 0)),
        pl.BlockSpec((None, SSM_DIM, SSM_STATE), lambda b, j: (b, 0, 0)),
    )
    scratch = [
        pltpu.VMEM((T, D_MODEL), BF16),
        pltpu.VMEM((D_MODEL // LANES, T + CONF_HALO, LANES), F32),
        pltpu.VMEM((D_MODEL // LANES, T, LANES), F32),
        pltpu.VMEM((SSM_XBC // LANES, T + SSM_HALO, LANES), F32),
        pltpu.VMEM((SSM_XBC // LANES, T, LANES), F32),
        pltpu.VMEM((SSM_STATE, SSM_DIM), F32),
        pltpu.VMEM((ZM_CHUNKS, T, ZM_COLS), F32),
        pltpu.VMEM((T, SSM_DIM), BF16),
    ]
    return pl.pallas_call(
        _prompt_kernel,
        grid=(bsz, nt),
        in_specs=in_specs,
        out_specs=out_specs,
        out_shape=out_shape,
        scratch_shapes=scratch,
        compiler_params=pltpu.CompilerParams(
            dimension_semantics=("arbitrary", "arbitrary"), vmem_limit_bytes=VMEM_LIMIT),
        name="prompt_layer",
    )(x, mod3, mod3, mod3, *consts)


def _sample_conv_state_kernel(cst_ref, sst_ref, cwsh_ref, swsh_ref, cs_out, xs_out):
    for b in range(cst_ref.shape[0]):
        st = cst_ref[b]
        for t in range(cs_out.shape[0]):
            cs_out[t, b:b + 1, :] = jnp.sum(st * cwsh_ref[t], axis=0, keepdims=True)
        sx = sst_ref[b]
        for t in range(xs_out.shape[0]):
            xs_out[t, b:b + 1, :] = jnp.sum(sx * swsh_ref[t], axis=0, keepdims=True)


def _sample_conv_state_call(cst, sst, cwsh, swsh):
    nb = cst.shape[0]
    sb = CONV_STATE_SEQ_BLOCK
    nt_c, nt_x = cwsh.shape[0], swsh.shape[0]
    return pl.pallas_call(
        _sample_conv_state_kernel,
        grid=(nb // sb,),
        in_specs=[pl.BlockSpec((sb,) + cst.shape[1:], lambda i: (i, 0, 0)),
                  pl.BlockSpec((sb,) + sst.shape[1:], lambda i: (i, 0, 0)),
                  _const(cwsh.shape, 1), _const(swsh.shape, 1)],
        out_specs=(pl.BlockSpec((nt_c, sb, cst.shape[2]), lambda i: (0, i, 0)),
                   pl.BlockSpec((nt_x, sb, sst.shape[2]), lambda i: (0, i, 0))),
        out_shape=(jax.ShapeDtypeStruct((nt_c, nb, cst.shape[2]), F32),
                   jax.ShapeDtypeStruct((nt_x, nb, sst.shape[2]), F32)),
        name="sample_conv_state",
    )(cst, sst, cwsh, swsh)


def _sample_pre_kernel(x_ref, sh_ref, sc_ref, cst_ref, sst_ref,
                       gpre_ref, cw_ref, cb_ref, lng_ref, lnb_ref, sw_ref, sb_ref, dtb_ref, alog_ref,
                       dexp_ref, e3_ref, wv_ref, wx_ref, wdt_ref, wco_ref,
                       h_out, bc_out, unew_out, xnew_out, yp_out, ec_out, c_out, b_out, xd_out, dec_out):
    nt, sb = x_ref.shape[0], x_ref.shape[1]
    rows = nt * sb

    def tile_rows(v):
        return jnp.concatenate([v] * nt, axis=0)

    x = x_ref[...].reshape(rows, D_MODEL)
    hb = _prenorm(x, gpre_ref[...], tile_rows(sc_ref[...]), tile_rows(sh_ref[...])).astype(BF16)
    h_out[...] = hb.reshape(nt, sb, D_MODEL)

    u = _dot(hb, _w(wv_ref, 0, D_MODEL)) * _sig(_dot(hb, _w(wv_ref, D_MODEL, 2 * D_MODEL)))
    unew_out[...] = u.reshape(nt, sb, D_MODEL)

    accs = []
    for t in range(nt):
        acc = cb_ref[...] + cst_ref[t]
        for k in range(CONV_WIDTH - 1 - t, CONV_WIDTH):
            r = t + k - (CONV_WIDTH - 1)
            acc = acc + cw_ref[k:k + 1, :] * u[r * sb:(r + 1) * sb, :]
        accs.append(acc)
    acc = jnp.concatenate(accs, axis=0)
    uc = _silu(_layernorm(acc, lng_ref[...], lnb_ref[...])) * _silu(_dot(hb, _w(wv_ref, 2 * D_MODEL, 3 * D_MODEL)))
    bc_out[...] = _dot(uc.astype(BF16), _w(wco_ref)).reshape(nt, sb, D_MODEL)

    xpre = _dot(hb, wx_ref[...])
    xnew_out[...] = xpre[(nt - (SSM_CONV_WIDTH - 1)) * sb:, :].reshape(SSM_CONV_WIDTH - 1, sb, SSM_XBC)

    xaccs = []
    for t in range(nt):
        xacc = jnp.broadcast_to(sb_ref[...], (sb, SSM_XBC))
        if t < SSM_CONV_WIDTH - 1:
            xacc = xacc + sst_ref[t]
        for k in range(max(SSM_CONV_WIDTH - 1 - t, 0), SSM_CONV_WIDTH):
            r = t + k - (SSM_CONV_WIDTH - 1)
            xacc = xacc + sw_ref[k:k + 1, :] * xpre[r * sb:(r + 1) * sb, :]
        xaccs.append(xacc)
    xbc = _silu(jnp.concatenate(xaccs, axis=0))
    xs = xbc[:, 0:SSM_DIM]
    bmf = xbc[:, SSM_DIM:SSM_DIM + SSM_BC]
    cmf = xbc[:, SSM_DIM + SSM_BC:]
    b_out[...] = bmf.reshape(nt, sb, SSM_BC)
    c_out[...] = cmf.reshape(nt, sb, SSM_BC)
    bm = bmf.astype(BF16)
    cm = cmf.astype(BF16)

    dt = _softplus(_dot(hb, _w(wdt_ref)) + dtb_ref[...])
    dta = dt * (-jnp.exp(alog_ref[...]))
    cums = [dta[0:sb, :]]
    for t in range(1, nt):
        cums.append(cums[-1] + dta[t * sb:(t + 1) * sb, :])
    cum = jnp.concatenate(cums, axis=0)
    cum_last = tile_rows(cums[-1])
    dec_out[...] = jnp.exp(cums[-1])
    cum_t = cum.T

    fac = jnp.concatenate([dt, jnp.exp(cum_last - cum), jnp.exp(cum)], axis=0)
    facx = _expand_heads(fac, _w(e3_ref))
    dt_x = facx[0:rows]
    dend_x = facx[rows:2 * rows]
    ec_out[...] = facx[2 * rows:3 * rows].reshape(nt, sb, SSM_DIM)
    xdt = xs * dt_x
    xdt_b = xdt.astype(BF16)
    xd_out[...] = (xdt * dend_x).reshape(nt, sb, SSM_DIM)

    row = lax.broadcasted_iota(jnp.int32, (rows, rows), 0)
    col = lax.broadcasted_iota(jnp.int32, (rows, rows), 1)
    mask = (col <= row) & ((col % sb) == (row % sb))
    ys = []
    for g in range(SSM_GROUPS):
        gs = slice(g * GROUP_DIM, (g + 1) * GROUP_DIM)
        ns = slice(g * SSM_STATE, (g + 1) * SSM_STATE)
        ys.append(_ssd_intra_group(cm[:, ns], bm[:, ns], cum, cum_t, mask, xdt_b[:, gs], g))
    yp = jnp.concatenate(ys, axis=1) + dexp_ref[...] * xs
    yp_out[...] = yp.reshape(nt, sb, SSM_DIM)


def _sample_pre_call(x_pm, mod, cst_t, sst_t, p):
    nt, nb, _ = x_pm.shape
    sb = SAMPLE_SEQ_BLOCK
    consts = (p["g_pre"], p["cw"], p["cb"], p["lng"], p["lnb"], p["sw"], p["sb"], p["dtb"], p["alog"],
              p["dexp"], p["e3"], p["w_v"], p["w_x"], p["w_dt"], p["w_co"])

    def pm(width, n=nt):
        return pl.BlockSpec((n, sb, width), lambda i: (0, i, 0))

    in_specs = ([pm(D_MODEL),
                 pl.BlockSpec((sb, D_MODEL), lambda i: (i, 0)),
                 pl.BlockSpec((sb, D_MODEL), lambda i: (i, 1)),
                 pm(D_MODEL), pm(SSM_XBC, SSM_CONV_WIDTH - 1)]
                + [_const(c.shape, 1) for c in consts])

    def sds(n, width, dtype=F32):
        return jax.ShapeDtypeStruct((n, nb, width), dtype)

    out_shape = (sds(nt, D_MODEL, BF16), sds(nt, D_MODEL), sds(nt, D_MODEL),
                 sds(SSM_CONV_WIDTH - 1, SSM_XBC), sds(nt, SSM_DIM), sds(nt, SSM_DIM),
                 sds(nt, SSM_BC), sds(nt, SSM_BC), sds(nt, SSM_DIM),
                 jax.ShapeDtypeStruct((nb, LANES), F32))
    out_specs = (pm(D_MODEL), pm(D_MODEL), pm(D_MODEL), pm(SSM_XBC, SSM_CONV_WIDTH - 1), pm(SSM_DIM),
                 pm(SSM_DIM), pm(SSM_BC), pm(SSM_BC), pm(SSM_DIM),
                 pl.BlockSpec((sb, LANES), lambda i: (i, 0)))
    return pl.pallas_call(
        _sample_pre_kernel,
        grid=(nb // sb,),
        in_specs=in_specs,
        out_specs=out_specs,
        out_shape=out_shape,
        compiler_params=pltpu.CompilerParams(
            dimension_semantics=("arbitrary",), vmem_limit_bytes=VMEM_LIMIT),
        name="sample_pre",
    )(x_pm, mod, mod, cst_t, sst_t, *consts)


def _sample_state_kernel(dec_ref, st_ref, c_ref, b_ref, xd_ref, st_out, yi_out):
    nt, sb = c_ref.shape[0], c_ref.shape[1]
    rows = nt * sb
    i = pl.program_id(0)
    cmat = c_ref[...].reshape(rows, SSM_BC).astype(BF16)
    bmat = b_ref[...].reshape(rows, SSM_BC)
    xd = xd_ref[...].reshape(rows, SSM_DIM)
    seq_of_row = lax.broadcasted_iota(jnp.int32, (rows, GROUP_DIM), 0) % sb
    seq_of_row_n = lax.broadcasted_iota(jnp.int32, (rows, SSM_STATE), 0) % sb
    accs = []
    for g in range(SSM_GROUPS):
        gs = slice(g * GROUP_DIM, (g + 1) * GROUP_DIM)
        ns = slice(g * SSM_STATE, (g + 1) * SSM_STATE)
        cg = cmat[:, ns]
        bg = bmat[:, ns]
        xd_t = xd[:, gs].T.astype(BF16)
        acc = jnp.zeros((rows, GROUP_DIM), F32)
        for b in range(sb):
            mine = seq_of_row == b
            h0 = st_ref[b, gs, :]
            acc = jnp.where(mine, _dot_nt(cg, h0.astype(BF16)), acc)
            b_mine = jnp.where(seq_of_row_n == b, bg, 0.0).astype(BF16)
            st = _dot(xd_t, b_mine)
            for r in range(HEADS_PER_GROUP):
                hs = slice(r * SSM_HEAD_DIM, (r + 1) * SSM_HEAD_DIM)
                dec = dec_ref[i * sb + b, g * HEADS_PER_GROUP + r]
                st_out[b, g * GROUP_DIM + r * SSM_HEAD_DIM:g * GROUP_DIM + (r + 1) * SSM_HEAD_DIM, :] = (
                    dec * h0[hs, :] + st[hs, :])
        accs.append(acc)
    yi_out[...] = jnp.concatenate(accs, axis=1).reshape(nt, sb, SSM_DIM)


def _sample_state_call(dec, state, c_pm, b_pm, xd_pm):
    nt, nb, _ = c_pm.shape
    sb = STATE_SEQ_BLOCK

    def pm(width):
        return pl.BlockSpec((nt, sb, width), lambda i: (0, i, 0))

    st_spec = pl.BlockSpec((sb, SSM_DIM, SSM_STATE), lambda i: (i, 0, 0))
    return pl.pallas_call(
        _sample_state_kernel,
        grid=(nb // sb,),
        in_specs=[pl.BlockSpec(memory_space=pltpu.SMEM), st_spec, pm(SSM_BC), pm(SSM_BC), pm(SSM_DIM)],
        out_specs=(st_spec, pm(SSM_DIM)),
        out_shape=(jax.ShapeDtypeStruct(state.shape, F32), jax.ShapeDtypeStruct((nt, nb, SSM_DIM), F32)),
        compiler_params=pltpu.CompilerParams(
            dimension_semantics=("arbitrary",), vmem_limit_bytes=VMEM_LIMIT),
        name="sample_state",
    )(dec, state, c_pm, b_pm, xd_pm)


def _sample_post_kernel(x_ref, gt_ref, h_ref, yp_ref, yi_ref, ec_ref, bc_ref, ng_ref, gpost_ref,
                        wzm_ref, wso_ref, wo_ref, y_out):
    reps = x_ref.shape[0] // gt_ref.shape[0]
    gate = jnp.concatenate([gt_ref[...]] * reps, axis=0)
    hb = h_ref[...]
    z = _dot(hb, wzm_ref[:, 0:SSM_DIM])
    m = _dot(hb, wzm_ref[:, SSM_DIM:])
    y = (yp_ref[...] + yi_ref[...] * ec_ref[...]) * _silu(z)
    branch_ssm = _dot(_group_rmsnorm(y, ng_ref[...]).astype(BF16), _w(wso_ref))
    merged = _sig(m[:, 0:D_MODEL]) * bc_ref[...] + _sig(m[:, D_MODEL:]) * branch_ssm
    o = _dot(merged.astype(BF16), _w(wo_ref))
    y_out[...] = _final(x_ref[...], gate, o, gpost_ref[...])


def _sample_post_call(x2, mod, h2, yp2, yi2, ec2, bc2, p):
    n = x2.shape[0]
    nb = mod.shape[0]
    rb = POST_ROWS
    consts = (p["ng"], p["g_post"], p["w_zm"], p["w_so"], p["w_o"])

    def rowblk(width):
        return pl.BlockSpec((rb, width), lambda i: (i, 0))

    in_specs = ([rowblk(D_MODEL), pl.BlockSpec((nb, D_MODEL), lambda i: (0, 2)), rowblk(D_MODEL),
                 rowblk(SSM_DIM), rowblk(SSM_DIM), rowblk(SSM_DIM), rowblk(D_MODEL)]
                + [_const(c.shape, 1) for c in consts])
    return pl.pallas_call(
        _sample_post_kernel,
        grid=(n // rb,),
        in_specs=in_specs,
        out_specs=rowblk(D_MODEL),
        out_shape=jax.ShapeDtypeStruct((n, D_MODEL), F32),
        compiler_params=pltpu.CompilerParams(
            dimension_semantics=("arbitrary",), vmem_limit_bytes=VMEM_LIMIT),
        name="sample_post",
    )(x2, mod, h2, yp2, yi2, ec2, bc2, *consts)


def _conf_roll_kernel(st_ref, u_ref, o_ref):
    keep = (CONV_WIDTH - 1) - u_ref.shape[1]
    o_ref[:, 0:keep, :] = st_ref[:, u_ref.shape[1]:, :]
    o_ref[:, keep:, :] = u_ref[...]


def _conf_roll_call(state, u_new):
    nb, hist, width = state.shape
    nt = u_new.shape[1]
    sb = STATE_SEQ_BLOCK
    return pl.pallas_call(
        _conf_roll_kernel,
        grid=(nb // sb,),
        in_specs=[pl.BlockSpec((sb, hist, width), lambda i: (i, 0, 0)),
                  pl.BlockSpec((sb, nt, width), lambda i: (i, 0, 0))],
        out_specs=pl.BlockSpec((sb, hist, width), lambda i: (i, 0, 0)),
        out_shape=jax.ShapeDtypeStruct(state.shape, F32),
        name="conf_roll",
    )(state, u_new)


def _selection_constants():
    tri = np.tril(np.ones((SSM_CHUNK, SSM_CHUNK), np.float32))
    tri3 = np.concatenate([tri, tri, tri], axis=1)
    e3 = np.zeros((LANES, SSM_DIM), np.float32)
    for piece in range(3):
        for hd in range(SSM_HEADS):
            e3[piece * SSM_HEADS + hd, hd * SSM_HEAD_DIM:(hd + 1) * SSM_HEAD_DIM] = 1.0
    return jnp.asarray(tri3, BF16), jnp.asarray(e3, BF16)


def _pad_lanes(v, n=LANES):
    return jnp.pad(v, ((0, 0), (0, n - v.shape[1])))


def kernel(x_prompt, x_sample, c_prompt, c_sample, state_conf_conv, state_ssm_conv, state_ssm, w_ada, b_ada, g_pre, g_post, w_in, conf_dw_w, conf_dw_b, conf_ln_g, conf_ln_b, w_conf_out, ssm_dw_w, ssm_dw_b, ssm_dt_bias, ssm_a_log, ssm_d, ssm_norm_g, w_ssm_out, w_o):
    bp = x_prompt.shape[0]
    nb, nt, _ = x_sample.shape
    l = 0
    mod = _ada(jnp.concatenate([c_prompt, c_sample], axis=0), w_ada[l], b_ada[l][None])

    wi = w_in[l]
    o_z = 3 * D_MODEL
    o_x = o_z + SSM_DIM
    o_dt = o_x + SSM_XBC
    o_m = o_dt + SSM_HEADS
    tri3, e3 = _selection_constants()
    def lane_slabs(w, rows):
        w = jnp.pad(w, ((0, rows - w.shape[0]), (0, 0)))
        return w.reshape(rows, w.shape[1] // LANES, LANES).transpose(1, 0, 2)

    cw = jnp.pad(conf_dw_w[l], ((0, 1), (0, 0)))
    p = {
        "w_v": wi[:, :o_z].astype(BF16),
        "w_x": wi[:, o_x:o_dt].astype(BF16),
        "w_zm": jnp.concatenate([wi[:, o_z:o_x], wi[:, o_m:]], axis=1).astype(BF16),
        "w_dt": _pad_lanes(wi[:, o_dt:o_m]).astype(BF16),
        "w_co": w_conf_out[l].astype(BF16), "w_so": w_ssm_out[l].astype(BF16), "w_o": w_o[l].astype(BF16),
        "g_pre": g_pre[l][None], "g_post": g_post[l][None],
        "cw": cw, "cb": conf_dw_b[l][None],
        "cw_s": lane_slabs(cw, cw.shape[0]), "cb_s": lane_slabs(conf_dw_b[l][None], 1),
        "lng": conf_ln_g[l][None], "lnb": conf_ln_b[l][None],
        "sw": ssm_dw_w[l], "sb": ssm_dw_b[l][None],
        "sw_s": lane_slabs(ssm_dw_w[l], SUBLANES), "sb_s": lane_slabs(ssm_dw_b[l][None], 1),
        "dtb": _pad_lanes(ssm_dt_bias[l][None]), "alog": _pad_lanes(ssm_a_log[l][None]),
        "dexp": jnp.repeat(ssm_d[l], SSM_HEAD_DIM)[None], "ng": ssm_norm_g[l][None],
        "tri3": tri3, "e3": e3,
    }

    yp, conf_p, ssmc_p, st_p = _prompt_call(x_prompt, mod[:bp], p)
    st_p = st_p.reshape(1, bp, SSM_HEADS, SSM_HEAD_DIM, SSM_STATE)

    mod_s = mod[bp:]
    x_pm = x_sample.transpose(1, 0, 2)
    cwsh = jnp.stack([jnp.pad(conf_dw_w[l][:CONV_WIDTH - 1 - t], ((t, 0), (0, 0))) for t in range(nt)])
    swsh = jnp.stack([jnp.pad(ssm_dw_w[l][:SSM_CONV_WIDTH - 1 - t], ((t, 0), (0, 0)))
                      for t in range(SSM_CONV_WIDTH - 1)])
    cs_pm, xs_pm = _sample_conv_state_call(state_conf_conv[l], state_ssm_conv[l], cwsh, swsh)
    h_pm, bc_pm, unew_pm, xnew_pm, yp_pm, ec_pm, c_pm, b_pm, xd_pm, dec = _sample_pre_call(
        x_pm, mod_s, cs_pm, xs_pm, p)
    st_s, yi_pm = _sample_state_call(dec[:, :SSM_HEADS], state_ssm[l].reshape(nb, SSM_DIM, SSM_STATE),
                                     c_pm, b_pm, xd_pm)

    def flat(v):
        return v.reshape(nt * nb, v.shape[-1])

    ys = _sample_post_call(flat(x_pm), mod_s, flat(h_pm), flat(yp_pm), flat(yi_pm), flat(ec_pm), flat(bc_pm), p)
    ys = ys.reshape(nt, nb, D_MODEL).transpose(1, 0, 2)
    conf_s = _conf_roll_call(state_conf_conv[l], unew_pm.transpose(1, 0, 2))
    ssmc_s = xnew_pm.transpose(1, 0, 2)
    st_s = st_s.reshape(1, nb, SSM_HEADS, SSM_HEAD_DIM, SSM_STATE)
    return (yp, ys, conf_p[None], ssmc_p[None], st_p, conf_s[None], ssmc_s[None], st_s)
```

```python
import functools

import jax
import jax.numpy as jnp
import numpy as np
from jax import lax
from jax.experimental import pallas as pl
from jax.experimental.pallas import tpu as pltpu

F32 = jnp.float32
BF16 = jnp.bfloat16

D_MODEL = 1024
CONV_WIDTH = 31
SSM_DIM = 2048
SSM_HEADS = 32
SSM_HEAD_DIM = 64
SSM_GROUPS = 8
SSM_STATE = 128
SSM_CONV_WIDTH = 4
SSM_BC = SSM_GROUPS * SSM_STATE
SSM_XBC = SSM_DIM + 2 * SSM_BC
SSM_CHUNK = 128
GROUP_DIM = SSM_DIM // SSM_GROUPS
HEADS_PER_GROUP = SSM_HEADS // SSM_GROUPS
EPS = 1e-6

SUBLANES = 8
LANES = 128
VMEM_LIMIT = 60 * 1024 * 1024

ZM_COLS = 512
ZM_CHUNKS = 2 * SSM_DIM // ZM_COLS

PROMPT_TILE = 256
CONF_HALO = 32
SSM_HALO = 8

CONV_STATE_SEQ_BLOCK = 16
SAMPLE_SEQ_BLOCK = 32
STATE_SEQ_BLOCK = 8
POST_ROWS = 256


def _sig(x):
    return 0.5 * jnp.tanh(0.5 * x) + 0.5


def _silu(x):
    return x * _sig(x)


def _w(ref):
    return ref[...]


def _chunk(ref, i):
    return ref[i]


def _cdot(a, w_ref, c0=0, c1=None):
    c1 = w_ref.shape[0] * ZM_COLS if c1 is None else c1
    return jnp.concatenate([_dot(a, w_ref[c]) for c in range(c0 // ZM_COLS, c1 // ZM_COLS)], axis=1)


def _dot(a, b):
    return jnp.dot(a, b, preferred_element_type=F32)


def _dot_nt(a, b):
    return lax.dot_general(a, b, (((1,), (1,)), ((), ())), preferred_element_type=F32)


def _dot_tn(a, b):
    return lax.dot_general(a, b, (((0,), (0,)), ((), ())), preferred_element_type=F32)


def _split3(v):
    hi = v.astype(BF16)
    r1 = v - hi.astype(F32)
    mid = r1.astype(BF16)
    lo = (r1 - mid.astype(F32)).astype(BF16)
    return hi, mid, lo


def _exact_left(sel3, v):
    hi, mid, lo = _split3(v)
    return _dot(sel3, jnp.concatenate([hi, mid, lo], axis=0))


def _expand_heads(v, e3):
    lane = lax.broadcasted_iota(jnp.int32, v.shape, 1)
    v = jnp.where(lane < SSM_HEADS, v, 0.0)
    hi = v.astype(BF16).astype(F32)
    r1 = v - hi
    mid = r1.astype(BF16).astype(F32)
    packed = hi + pltpu.roll(mid, SSM_HEADS, 1) + pltpu.roll(r1 - mid, 2 * SSM_HEADS, 1)
    return _dot(packed.astype(BF16), e3)


def _softplus(x):
    return jnp.maximum(x, 0.0) + jnp.log1p(jnp.exp(-jnp.abs(x)))


def _prenorm(x, g_pre, scale, shift):
    r = lax.rsqrt(jnp.mean(x * x, axis=-1, keepdims=True) + EPS)
    return ((x * r) * g_pre) * (1.0 + scale) + shift


def _layernorm(x, g, b):
    mu = jnp.mean(x, axis=-1, keepdims=True)
    xc = x - mu
    var = jnp.mean(xc * xc, axis=-1, keepdims=True)
    return (xc * lax.rsqrt(var + EPS)) * g + b


def _group_rmsnorm(y, g):
    outs = []
    for gi in range(SSM_GROUPS):
        yg = y[:, gi * GROUP_DIM:(gi + 1) * GROUP_DIM]
        r = lax.rsqrt(jnp.mean(yg * yg, axis=-1, keepdims=True) + EPS)
        outs.append((yg * r) * g[:, gi * GROUP_DIM:(gi + 1) * GROUP_DIM])
    return jnp.concatenate(outs, axis=1)


def _final(x, gate, o, g_post):
    r = lax.rsqrt(jnp.mean(o * o, axis=-1, keepdims=True) + EPS)
    return x + gate * ((o * r) * g_post)


CONV_ROW_STRIDE = 4
CONV_WINDOW = 128


def _to_slabs(buf, v, row0, slab0):
    for s in range(v.shape[1] // LANES):
        buf[slab0 + s, row0:row0 + v.shape[0], :] = v[:, s * LANES:(s + 1) * LANES]


def _conv_slab(buf, w_ref, b_ref, out, s, width, halo, rows, act):
    base = halo - (width - 1)
    n = CONV_WINDOW // CONV_ROW_STRIDE
    for w0 in range(0, rows, CONV_WINDOW):
        accs = [jnp.broadcast_to(b_ref[s], (n, LANES)) for _ in range(CONV_ROW_STRIDE)]
        for d in range(width + CONV_ROW_STRIDE - 1):
            v = buf[s, pl.ds(base + w0 + d, n, stride=CONV_ROW_STRIDE), :]
            for o in range(CONV_ROW_STRIDE):
                if 0 <= d - o < width:
                    accs[o] = accs[o] + jnp.broadcast_to(w_ref[s, d - o:d - o + 1, :], (n, LANES)) * v
        for o in range(CONV_ROW_STRIDE):
            out[s, pl.ds(w0 + o, n, stride=CONV_ROW_STRIDE), :] = act(accs[o])


def _carry_slabs(buf, tail_out, width, halo, rows):
    for s in range(buf.shape[0]):
        tail_out[:, s * LANES:(s + 1) * LANES] = buf[s, rows + halo - (width - 1):rows + halo, :]
        buf[s, 0:halo, :] = buf[s, rows:rows + halo, :]


def _ssd_intra_group(cg, bg, cum, cum_t, mask, xdt_g, g):
    rows = cg.shape[0]
    cb = _dot_nt(cg, bg)
    lane = lax.broadcasted_iota(jnp.int32, (rows, GROUP_DIM), 1)
    ms, xbd = [], []
    for r in range(HEADS_PER_GROUP):
        hd = g * HEADS_PER_GROUP + r
        seg = cum[:, hd:hd + 1] - cum_t[hd:hd + 1, :]
        dec = jnp.exp(jnp.where(mask, seg, -jnp.inf))
        ms.append((cb * dec).astype(BF16))
        in_head = (lane >= r * SSM_HEAD_DIM) & (lane < (r + 1) * SSM_HEAD_DIM)
        xbd.append(jnp.where(in_head, xdt_g, jnp.zeros_like(xdt_g)))
    return _dot(jnp.concatenate(ms, axis=1), jnp.concatenate(xbd, axis=0))


def _ada_kernel(c_ref, w_ref, b_ref, o_ref):
    s = _silu(c_ref[...]).astype(BF16)
    o_ref[...] = _dot(s, w_ref[...].astype(BF16)) + b_ref[...]


def _ada(c_all, w_ada, b_ada):
    n = c_all.shape[0]
    return pl.pallas_call(
        _ada_kernel,
        grid=(3,),
        in_specs=[
            pl.BlockSpec((n, D_MODEL), lambda k: (0, 0)),
            pl.BlockSpec((D_MODEL, D_MODEL), lambda k: (0, k)),
            pl.BlockSpec((1, D_MODEL), lambda k: (0, k)),
        ],
        out_specs=pl.BlockSpec((n, D_MODEL), lambda k: (0, k)),
        out_shape=jax.ShapeDtypeStruct((n, 3 * D_MODEL), F32),
        name="ada_mod",
    )(c_all, w_ada, b_ada)


def _prompt_kernel(x_ref, sh_ref, sc_ref, gt_ref, gpre_ref, gpost_ref, cw_ref, cb_ref, lng_ref, lnb_ref,
                   sw_ref, sb_ref, dtb_ref, alog_ref, dexp_ref, ng_ref, tri_ref, e3_ref,
                   wv_ref, wx_ref, wzm_ref, wdt_ref, wco_ref, wso_ref, wo_ref,
                   y_out, conf_out, ssmc_out, st_out,
                   h_ref, ubuf, cacc, xbuf, xact, ht_ref, zm_ref, yn_ref):
    T = PROMPT_TILE
    j = pl.program_id(1)

    @pl.when(j == 0)
    def _():
        ubuf[:, 0:CONF_HALO, :] = jnp.zeros((D_MODEL // LANES, CONF_HALO, LANES), F32)
        xbuf[:, 0:SSM_HALO, :] = jnp.zeros((SSM_XBC // LANES, SSM_HALO, LANES), F32)
        ht_ref[...] = jnp.zeros_like(ht_ref)

    h_ref[...] = _prenorm(x_ref[...], gpre_ref[...], sc_ref[...], sh_ref[...]).astype(BF16)

    u = _cdot(h_ref[...], wv_ref, 0, D_MODEL) * _sig(_cdot(h_ref[...], wv_ref, D_MODEL, 2 * D_MODEL))
    _to_slabs(ubuf, u, CONF_HALO, 0)
    c_slabs = D_MODEL // LANES
    spc = ZM_COLS // LANES
    assert c_slabs == ZM_CHUNKS == SSM_XBC // ZM_COLS

    def project_xbc(i):
        _to_slabs(xbuf, _dot(h_ref[...], _chunk(wx_ref, i)), SSM_HALO, i * spc)

    project_xbc(0)
    for i in range(ZM_CHUNKS):
        _conv_slab(ubuf, cw_ref, cb_ref, cacc, i, CONV_WIDTH, CONF_HALO, T, lambda v: v)
        for q in range(spc):
            _conv_slab(xbuf, sw_ref, sb_ref, xact, i * spc + q, SSM_CONV_WIDTH, SSM_HALO, T, _silu)
        if i + 1 < ZM_CHUNKS:
            project_xbc(i + 1)
        zm_ref[i] = (_silu if i < ZM_CHUNKS // 2 else _sig)(_dot(h_ref[...], _chunk(wzm_ref, i)))
    _carry_slabs(ubuf, conf_out, CONV_WIDTH, CONF_HALO, T)
    _carry_slabs(xbuf, ssmc_out, SSM_CONV_WIDTH, SSM_HALO, T)
    acc = jnp.concatenate([cacc[s] for s in range(c_slabs)], axis=1)
    uc = _silu(_layernorm(acc, lng_ref[...], lnb_ref[...])) * _silu(
        _cdot(h_ref[...], wv_ref, 2 * D_MODEL, 3 * D_MODEL))
    branch_conv = _cdot(uc.astype(BF16), wco_ref)
    xs_slabs = SSM_DIM // LANES
    b_slab0 = xs_slabs
    c_slab0 = xs_slabs + SSM_GROUPS
    slabs_per_group = GROUP_DIM // LANES

    dt = _softplus(_dot(h_ref[...], _w(wdt_ref)) + dtb_ref[...])
    dta = dt * (-jnp.exp(alog_ref[...]))

    row = lax.broadcasted_iota(jnp.int32, (SSM_CHUNK, SSM_CHUNK), 0)
    col = lax.broadcasted_iota(jnp.int32, (SSM_CHUNK, SSM_CHUNK), 1)
    causal = col <= row

    for c in range(T // SSM_CHUNK):
        rs = slice(c * SSM_CHUNK, (c + 1) * SSM_CHUNK)
        cum = _exact_left(_w(tri_ref), dta[rs, :])
        cum_last = cum[SSM_CHUNK - 1:SSM_CHUNK, :]
        cum_t = cum.T
        fac = jnp.concatenate([
            dt[rs, :],
            jnp.exp(cum_last - cum),
            jnp.exp(cum),
            jnp.broadcast_to(jnp.exp(cum_last), (SUBLANES, LANES)),
        ], axis=0)
        facx = _expand_heads(fac, _w(e3_ref))
        dt_x = facx[0:SSM_CHUNK]
        dend_x = facx[SSM_CHUNK:2 * SSM_CHUNK]
        ecum_x = facx[2 * SSM_CHUNK:3 * SSM_CHUNK]
        cdec_x = facx[3 * SSM_CHUNK:3 * SSM_CHUNK + 1]
        xdt = jnp.concatenate([xact[s, rs, :] for s in range(xs_slabs)], axis=1) * dt_x
        xdt_b = xdt.astype(BF16)
        xdte_b = (xdt * dend_x).astype(BF16)
        for g in range(SSM_GROUPS):
            gs = slice(g * GROUP_DIM, (g + 1) * GROUP_DIM)
            ns = slice(g * SSM_STATE, (g + 1) * SSM_STATE)
            bg = xact[b_slab0 + g, rs, :].astype(BF16)
            cg = xact[c_slab0 + g, rs, :].astype(BF16)
            htg = ht_ref[:, gs]
            y_g = _dot(cg, htg.astype(BF16)) * ecum_x[:, gs]
            y_g = y_g + _ssd_intra_group(cg, bg, cum, cum_t, causal, xdt_b[:, gs], g)
            _to_slabs(xbuf, y_g, SSM_HALO + c * SSM_CHUNK, g * slabs_per_group)
            ht_ref[:, gs] = cdec_x[:, gs] * htg + _dot_tn(bg, xdte_b[:, gs])

    for g in range(SSM_GROUPS):
        gs = slice(g * GROUP_DIM, (g + 1) * GROUP_DIM)
        sl = range(g * slabs_per_group, (g + 1) * slabs_per_group)
        y_g = jnp.concatenate([xbuf[s, SSM_HALO:SSM_HALO + T, :] for s in sl], axis=1)
        xs_g = jnp.concatenate([xact[s] for s in sl], axis=1)
        zc = g * GROUP_DIM % ZM_COLS
        zs_g = zm_ref[g * GROUP_DIM // ZM_COLS, :, zc:zc + GROUP_DIM]
        y_g = (y_g + dexp_ref[:, gs] * xs_g) * zs_g
        r = lax.rsqrt(jnp.mean(y_g * y_g, axis=-1, keepdims=True) + EPS)
        yn_ref[:, gs] = ((y_g * r) * ng_ref[:, gs]).astype(BF16)
    branch_ssm = _cdot(yn_ref[...], wso_ref)

    m0 = SSM_DIM // ZM_COLS
    per = D_MODEL // ZM_COLS
    mg_conv = jnp.concatenate([zm_ref[m0 + q] for q in range(per)], axis=1)
    mg_ssm = jnp.concatenate([zm_ref[m0 + per + q] for q in range(per)], axis=1)
    merged = mg_conv * branch_conv + mg_ssm * branch_ssm
    o = _cdot(merged.astype(BF16), wo_ref)
    y_out[...] = _final(x_ref[...], gt_ref[...], o, gpost_ref[...])

    @pl.when(j == pl.num_programs(1) - 1)
    def _():
        st_out[...] = ht_ref[...].T


def _const(shape, ngrid):
    nd = len(shape)
    return pl.BlockSpec(shape, lambda *_: (0,) * nd, pipeline_mode=pl.Buffered(1))


def _prompt_call(x, mod, p):
    bsz, seq, _ = x.shape
    T = PROMPT_TILE
    nt = seq // T
    mod3 = mod.reshape(bsz, 1, 3 * D_MODEL)
    consts = (p["g_pre"], p["g_post"], p["cw_s"], p["cb_s"], p["lng"], p["lnb"], p["sw_s"], p["sb_s"], p["dtb"],
              p["alog"], p["dexp"], p["ng"], p["tri3"], p["e3"],
              p["w_v"], p["w_x"], p["w_zm"], p["w_dt"], p["w_co"], p["w_so"], p["w_o"])
    mod_specs = [pl.BlockSpec((None, 1, D_MODEL), functools.partial(lambda k, b, j: (b, 0, k), k))
                 for k in range(3)]
    in_specs = ([pl.BlockSpec((None, T, D_MODEL), lambda b, j: (b, j, 0))] + mod_specs
                + [_const(c.shape, 2) for c in consts])
    out_shape = (
        jax.ShapeDtypeStruct((bsz, seq, D_MODEL), F32),
        jax.ShapeDtypeStruct((bsz, CONV_WIDTH - 1, D_MODEL), F32),
        jax.ShapeDtypeStruct((bsz, SSM_CONV_WIDTH - 1, SSM_XBC), F32),
        jax.ShapeDtypeStruct((bsz, SSM_DIM, SSM_STATE), F32),
    )
    out_specs = (
        pl.BlockSpec((None, T, D_MODEL), lambda b, j: (b, j, 0)),
        pl.BlockSpec((None, CONV_WIDTH - 1, D_MODEL), lambda b, j: (b, 0, 0)),
        pl.BlockSpec((None, SSM_CONV_WIDTH - 1, SSM_XBC), lambda b, j: (b, 0, 0)),
        pl.BlockSpec((None, SSM_DIM, SSM_STATE), lambda b, j: (b, 0, 0)),
    )
    scratch = [
        pltpu.VMEM((T, D_MODEL), BF16),
        pltpu.VMEM((D_MODEL // LANES, T + CONF_HALO, LANES), F32),
        pltpu.VMEM((D_MODEL // LANES, T, LANES), F32),
        pltpu.VMEM((SSM_XBC // LANES, T + SSM_HALO, LANES), F32),
        pltpu.VMEM((SSM_XBC // LANES, T, LANES), F32),
        pltpu.VMEM((SSM_STATE, SSM_DIM), F32),
        pltpu.VMEM((ZM_CHUNKS, T, ZM_COLS), F32),
        pltpu.VMEM((T, SSM_DIM), BF16),
    ]
    return pl.pallas_call(
        _prompt_kernel,
        grid=(bsz, nt),
        in_specs=in_specs,
        out_specs=out_specs,
        out_shape=out_shape,
        scratch_shapes=scratch,
        compiler_params=pltpu.CompilerParams(
            dimension_semantics=("arbitrary", "arbitrary"), vmem_limit_bytes=VMEM_LIMIT),
        name="prompt_layer",
    )(x, mod3, mod3, mod3, *consts)


def _sample_conv_state_kernel(cst_ref, sst_ref, cwsh_ref, swsh_ref, cs_out, xs_out):
    for b in range(cst_ref.shape[0]):
        st = cst_ref[b]
        for t in range(cs_out.shape[0]):
            cs_out[t, b:b + 1, :] = jnp.sum(st * cwsh_ref[t], axis=0, keepdims=True)
        sx = sst_ref[b]
        for t in range(xs_out.shape[0]):
            xs_out[t, b:b + 1, :] = jnp.sum(sx * swsh_ref[t], axis=0, keepdims=True)


def _sample_conv_state_call(cst, sst, layer, cwsh, swsh):
    nb = cst.shape[1]
    sb = CONV_STATE_SEQ_BLOCK
    nt_c, nt_x = cwsh.shape[0], swsh.shape[0]
    return pl.pallas_call(
        _sample_conv_state_kernel,
        grid=(nb // sb,),
        in_specs=[pl.BlockSpec((None, sb) + cst.shape[2:], lambda i: (layer, i, 0, 0)),
                  pl.BlockSpec((None, sb) + sst.shape[2:], lambda i: (layer, i, 0, 0)),
                  _const(cwsh.shape, 1), _const(swsh.shape, 1)],
        out_specs=(pl.BlockSpec((nt_c, sb, cst.shape[3]), lambda i: (0, i, 0)),
                   pl.BlockSpec((nt_x, sb, sst.shape[3]), lambda i: (0, i, 0))),
        out_shape=(jax.ShapeDtypeStruct((nt_c, nb, cst.shape[3]), F32),
                   jax.ShapeDtypeStruct((nt_x, nb, sst.shape[3]), F32)),
        name="sample_conv_state",
    )(cst, sst, cwsh, swsh)


def _sample_pre_kernel(x_ref, sh_ref, sc_ref, cst_ref, sst_ref,
                       gpre_ref, cw_ref, cb_ref, lng_ref, lnb_ref, sw_ref, sb_ref, dtb_ref, alog_ref,
                       dexp_ref, e3_ref, wv_ref, wx_ref, wdt_ref, wco_ref,
                       h_out, bc_out, unew_out, xnew_out, yp_out, ec_out, c_out, b_out, xd_out, dec_out):
    nt, sb = x_ref.shape[0], x_ref.shape[1]
    rows = nt * sb

    def tile_rows(v):
        return jnp.concatenate([v] * nt, axis=0)

    x = x_ref[...].reshape(rows, D_MODEL)
    hb = _prenorm(x, gpre_ref[...], tile_rows(sc_ref[...]), tile_rows(sh_ref[...])).astype(BF16)
    h_out[...] = hb.reshape(nt, sb, D_MODEL)

    u = _cdot(hb, wv_ref, 0, D_MODEL) * _sig(_cdot(hb, wv_ref, D_MODEL, 2 * D_MODEL))
    unew_out[...] = u.reshape(nt, sb, D_MODEL)

    accs = []
    for t in range(nt):
        acc = cb_ref[...] + cst_ref[t]
        for k in range(CONV_WIDTH - 1 - t, CONV_WIDTH):
            r = t + k - (CONV_WIDTH - 1)
            acc = acc + cw_ref[k:k + 1, :] * u[r * sb:(r + 1) * sb, :]
        accs.append(acc)
    acc = jnp.concatenate(accs, axis=0)
    uc = _silu(_layernorm(acc, lng_ref[...], lnb_ref[...])) * _silu(_cdot(hb, wv_ref, 2 * D_MODEL, 3 * D_MODEL))
    bc_out[...] = _cdot(uc.astype(BF16), wco_ref).reshape(nt, sb, D_MODEL)

    xpre = _cdot(hb, wx_ref)
    xnew_out[...] = xpre[(nt - (SSM_CONV_WIDTH - 1)) * sb:, :].reshape(SSM_CONV_WIDTH - 1, sb, SSM_XBC)

    xaccs = []
    for t in range(nt):
        xacc = jnp.broadcast_to(sb_ref[...], (sb, SSM_XBC))
        if t < SSM_CONV_WIDTH - 1:
            xacc = xacc + sst_ref[t]
        for k in range(max(SSM_CONV_WIDTH - 1 - t, 0), SSM_CONV_WIDTH):
            r = t + k - (SSM_CONV_WIDTH - 1)
            xacc = xacc + sw_ref[k:k + 1, :] * xpre[r * sb:(r + 1) * sb, :]
        xaccs.append(xacc)
    xbc = _silu(jnp.concatenate(xaccs, axis=0))
    xs = xbc[:, 0:SSM_DIM]
    bmf = xbc[:, SSM_DIM:SSM_DIM + SSM_BC]
    cmf = xbc[:, SSM_DIM + SSM_BC:]
    b_out[...] = bmf.reshape(nt, sb, SSM_BC)
    c_out[...] = cmf.reshape(nt, sb, SSM_BC)
    bm = bmf.astype(BF16)
    cm = cmf.astype(BF16)

    dt = _softplus(_dot(hb, _w(wdt_ref)) + dtb_ref[...])
    dta = dt * (-jnp.exp(alog_ref[...]))
    cums = [dta[0:sb, :]]
    for t in range(1, nt):
        cums.append(cums[-1] + dta[t * sb:(t + 1) * sb, :])
    cum = jnp.concatenate(cums, axis=0)
    cum_last = tile_rows(cums[-1])
    dec_out[...] = jnp.exp(cums[-1])
    cum_t = cum.T

    fac = jnp.concatenate([dt, jnp.exp(cum_last - cum), jnp.exp(cum)], axis=0)
    facx = _expand_heads(fac, _w(e3_ref))
    dt_x = facx[0:rows]
    dend_x = facx[rows:2 * rows]
    ec_out[...] = facx[2 * rows:3 * rows].reshape(nt, sb, SSM_DIM)
    xdt = xs * dt_x
    xdt_b = xdt.astype(BF16)
    xd_out[...] = (xdt * dend_x).reshape(nt, sb, SSM_DIM)

    row = lax.broadcasted_iota(jnp.int32, (rows, rows), 0)
    col = lax.broadcasted_iota(jnp.int32, (rows, rows), 1)
    mask = (col <= row) & ((col % sb) == (row % sb))
    ys = []
    for g in range(SSM_GROUPS):
        gs = slice(g * GROUP_DIM, (g + 1) * GROUP_DIM)
        ns = slice(g * SSM_STATE, (g + 1) * SSM_STATE)
        ys.append(_ssd_intra_group(cm[:, ns], bm[:, ns], cum, cum_t, mask, xdt_b[:, gs], g))
    yp = jnp.concatenate(ys, axis=1) + dexp_ref[...] * xs
    yp_out[...] = yp.reshape(nt, sb, SSM_DIM)


def _sample_pre_call(x_pm, mod, cst_t, sst_t, p):
    nt, nb, _ = x_pm.shape
    sb = SAMPLE_SEQ_BLOCK
    consts = (p["g_pre"], p["cw"], p["cb"], p["lng"], p["lnb"], p["sw"], p["sb"], p["dtb"], p["alog"],
              p["dexp"], p["e3"], p["w_v"], p["w_x"], p["w_dt"], p["w_co"])

    def pm(width, n=nt):
        return pl.BlockSpec((n, sb, width), lambda i: (0, i, 0))

    in_specs = ([pm(D_MODEL),
                 pl.BlockSpec((sb, D_MODEL), lambda i: (i, 0)),
                 pl.BlockSpec((sb, D_MODEL), lambda i: (i, 1)),
                 pm(D_MODEL), pm(SSM_XBC, SSM_CONV_WIDTH - 1)]
                + [_const(c.shape, 1) for c in consts])

    def sds(n, width, dtype=F32):
        return jax.ShapeDtypeStruct((n, nb, width), dtype)

    out_shape = (sds(nt, D_MODEL, BF16), sds(nt, D_MODEL), sds(nt, D_MODEL),
                 sds(SSM_CONV_WIDTH - 1, SSM_XBC), sds(nt, SSM_DIM), sds(nt, SSM_DIM),
                 sds(nt, SSM_BC), sds(nt, SSM_BC), sds(nt, SSM_DIM),
                 jax.ShapeDtypeStruct((nb, LANES), F32))
    out_specs = (pm(D_MODEL), pm(D_MODEL), pm(D_MODEL), pm(SSM_XBC, SSM_CONV_WIDTH - 1), pm(SSM_DIM),
                 pm(SSM_DIM), pm(SSM_BC), pm(SSM_BC), pm(SSM_DIM),
                 pl.BlockSpec((sb, LANES), lambda i: (i, 0)))
    return pl.pallas_call(
        _sample_pre_kernel,
        grid=(nb // sb,),
        in_specs=in_specs,
        out_specs=out_specs,
        out_shape=out_shape,
        compiler_params=pltpu.CompilerParams(
            dimension_semantics=("arbitrary",), vmem_limit_bytes=VMEM_LIMIT),
        name="sample_pre",
    )(x_pm, mod, mod, cst_t, sst_t, *consts)


def _sample_state_kernel(dec_ref, st_ref, c_ref, b_ref, xd_ref, st_out, yi_out):
    nt, sb = c_ref.shape[0], c_ref.shape[1]
    rows = nt * sb
    i = pl.program_id(0)
    cmat = c_ref[...].reshape(rows, SSM_BC).astype(BF16)
    bmat = b_ref[...].reshape(rows, SSM_BC)
    xd = xd_ref[...].reshape(rows, SSM_DIM)
    seq_of_row = lax.broadcasted_iota(jnp.int32, (rows, GROUP_DIM), 0) % sb
    seq_of_row_n = lax.broadcasted_iota(jnp.int32, (rows, SSM_STATE), 0) % sb
    accs = []
    for g in range(SSM_GROUPS):
        gs = slice(g * GROUP_DIM, (g + 1) * GROUP_DIM)
        ns = slice(g * SSM_STATE, (g + 1) * SSM_STATE)
        cg = cmat[:, ns]
        bg = bmat[:, ns]
        xd_t = xd[:, gs].T.astype(BF16)
        acc = jnp.zeros((rows, GROUP_DIM), F32)
        for b in range(sb):
            mine = seq_of_row == b
            h0 = st_ref[b, gs, :]
            acc = jnp.where(mine, _dot_nt(cg, h0.astype(BF16)), acc)
            b_mine = jnp.where(seq_of_row_n == b, bg, 0.0).astype(BF16)
            st = _dot(xd_t, b_mine)
            for r in range(HEADS_PER_GROUP):
                hs = slice(r * SSM_HEAD_DIM, (r + 1) * SSM_HEAD_DIM)
                dec = dec_ref[i * sb + b, g * HEADS_PER_GROUP + r]
                st_out[b, g * GROUP_DIM + r * SSM_HEAD_DIM:g * GROUP_DIM + (r + 1) * SSM_HEAD_DIM, :] = (
                    dec * h0[hs, :] + st[hs, :])
        accs.append(acc)
    yi_out[...] = jnp.concatenate(accs, axis=1).reshape(nt, sb, SSM_DIM)


def _sample_state_call(dec, state, c_pm, b_pm, xd_pm):
    nt, nb, _ = c_pm.shape
    sb = STATE_SEQ_BLOCK

    def pm(width):
        return pl.BlockSpec((nt, sb, width), lambda i: (0, i, 0))

    st_spec = pl.BlockSpec((sb, SSM_DIM, SSM_STATE), lambda i: (i, 0, 0))
    return pl.pallas_call(
        _sample_state_kernel,
        grid=(nb // sb,),
        in_specs=[pl.BlockSpec(memory_space=pltpu.SMEM), st_spec, pm(SSM_BC), pm(SSM_BC), pm(SSM_DIM)],
        out_specs=(st_spec, pm(SSM_DIM)),
        out_shape=(jax.ShapeDtypeStruct(state.shape, F32), jax.ShapeDtypeStruct((nt, nb, SSM_DIM), F32)),
        compiler_params=pltpu.CompilerParams(
            dimension_semantics=("arbitrary",), vmem_limit_bytes=VMEM_LIMIT),
        name="sample_state",
    )(dec, state, c_pm, b_pm, xd_pm)


def _sample_post_kernel(x_ref, gt_ref, h_ref, yp_ref, yi_ref, ec_ref, bc_ref, ng_ref, gpost_ref,
                        wzm_ref, wso_ref, wo_ref, y_out):
    reps = x_ref.shape[0] // gt_ref.shape[0]
    gate = jnp.concatenate([gt_ref[...]] * reps, axis=0)
    hb = h_ref[...]
    z = _cdot(hb, wzm_ref, 0, SSM_DIM)
    m = _cdot(hb, wzm_ref, SSM_DIM)
    y = (yp_ref[...] + yi_ref[...] * ec_ref[...]) * _silu(z)
    branch_ssm = _cdot(_group_rmsnorm(y, ng_ref[...]).astype(BF16), wso_ref)
    merged = _sig(m[:, 0:D_MODEL]) * bc_ref[...] + _sig(m[:, D_MODEL:]) * branch_ssm
    o = _cdot(merged.astype(BF16), wo_ref)
    y_out[...] = _final(x_ref[...], gate, o, gpost_ref[...])


def _sample_post_call(x2, mod, h2, yp2, yi2, ec2, bc2, p):
    n = x2.shape[0]
    nb = mod.shape[0]
    rb = POST_ROWS
    consts = (p["ng"], p["g_post"], p["w_zm"], p["w_so"], p["w_o"])

    def rowblk(width):
        return pl.BlockSpec((rb, width), lambda i: (i, 0))

    in_specs = ([rowblk(D_MODEL), pl.BlockSpec((nb, D_MODEL), lambda i: (0, 2)), rowblk(D_MODEL),
                 rowblk(SSM_DIM), rowblk(SSM_DIM), rowblk(SSM_DIM), rowblk(D_MODEL)]
                + [_const(c.shape, 1) for c in consts])
    return pl.pallas_call(
        _sample_post_kernel,
        grid=(n // rb,),
        in_specs=in_specs,
        out_specs=rowblk(D_MODEL),
        out_shape=jax.ShapeDtypeStruct((n, D_MODEL), F32),
        compiler_params=pltpu.CompilerParams(
            dimension_semantics=("arbitrary",), vmem_limit_bytes=VMEM_LIMIT),
        name="sample_post",
    )(x2, mod, h2, yp2, yi2, ec2, bc2, *consts)


def _conf_roll_kernel(st_ref, u_ref, o_ref):
    keep = (CONV_WIDTH - 1) - u_ref.shape[1]
    o_ref[:, 0:keep, :] = st_ref[:, u_ref.shape[1]:, :]
    o_ref[:, keep:, :] = u_ref[...]


def _conf_roll_call(state, layer, u_new):
    _, nb, hist, width = state.shape
    nt = u_new.shape[1]
    sb = STATE_SEQ_BLOCK
    return pl.pallas_call(
        _conf_roll_kernel,
        grid=(nb // sb,),
        in_specs=[pl.BlockSpec((None, sb, hist, width), lambda i: (layer, i, 0, 0)),
                  pl.BlockSpec((sb, nt, width), lambda i: (i, 0, 0))],
        out_specs=pl.BlockSpec((None, sb, hist, width), lambda i: (0, i, 0, 0)),
        out_shape=jax.ShapeDtypeStruct((1, nb, hist, width), F32),
        name="conf_roll",
    )(state, u_new)


def _selection_constants():
    tri = np.tril(np.ones((SSM_CHUNK, SSM_CHUNK), np.float32))
    tri3 = np.concatenate([tri, tri, tri], axis=1)
    e3 = np.zeros((LANES, SSM_DIM), np.float32)
    for piece in range(3):
        for hd in range(SSM_HEADS):
            e3[piece * SSM_HEADS + hd, hd * SSM_HEAD_DIM:(hd + 1) * SSM_HEAD_DIM] = 1.0
    return jnp.asarray(tri3, BF16), jnp.asarray(e3, BF16)


def _pad_lanes(v, n=LANES):
    return jnp.pad(v, ((0, 0), (0, n - v.shape[1])))


def kernel(x_prompt, x_sample, c_prompt, c_sample, state_conf_conv, state_ssm_conv, state_ssm, w_ada, b_ada, g_pre, g_post, w_in, conf_dw_w, conf_dw_b, conf_ln_g, conf_ln_b, w_conf_out, ssm_dw_w, ssm_dw_b, ssm_dt_bias, ssm_a_log, ssm_d, ssm_norm_g, w_ssm_out, w_o):
    bp = x_prompt.shape[0]
    nb, nt, _ = x_sample.shape
    l = 0
    mod = _ada(jnp.concatenate([c_prompt, c_sample], axis=0), w_ada[l], b_ada[l][None])

    wi = w_in[l]
    o_z = 3 * D_MODEL
    o_x = o_z + SSM_DIM
    o_dt = o_x + SSM_XBC
    o_m = o_dt + SSM_HEADS
    tri3, e3 = _selection_constants()

    def col_chunks(w):
        return w.astype(BF16).reshape(w.shape[0], w.shape[1] // ZM_COLS, ZM_COLS).transpose(1, 0, 2)

    def lane_slabs(w, rows):
        w = jnp.pad(w, ((0, rows - w.shape[0]), (0, 0)))
        return w.reshape(rows, w.shape[1] // LANES, LANES).transpose(1, 0, 2)

    cw = jnp.pad(conf_dw_w[l], ((0, 1), (0, 0)))
    p = {
        "w_v": col_chunks(wi[:, :o_z]),
        "w_x": col_chunks(wi[:, o_x:o_dt]),
        "w_zm": col_chunks(jnp.concatenate([wi[:, o_z:o_x], wi[:, o_m:]], axis=1)),
        "w_dt": _pad_lanes(wi[:, o_dt:o_m]).astype(BF16),
        "w_co": col_chunks(w_conf_out[l]), "w_so": col_chunks(w_ssm_out[l]), "w_o": col_chunks(w_o[l]),
        "g_pre": g_pre[l][None], "g_post": g_post[l][None],
        "cw": cw, "cb": conf_dw_b[l][None],
        "cw_s": lane_slabs(cw, cw.shape[0]), "cb_s": lane_slabs(conf_dw_b[l][None], 1),
        "lng": conf_ln_g[l][None], "lnb": conf_ln_b[l][None],
        "sw": ssm_dw_w[l], "sb": ssm_dw_b[l][None],
        "sw_s": lane_slabs(ssm_dw_w[l], SUBLANES), "sb_s": lane_slabs(ssm_dw_b[l][None], 1),
        "dtb": _pad_lanes(ssm_dt_bias[l][None]), "alog": _pad_lanes(ssm_a_log[l][None]),
        "dexp": jnp.repeat(ssm_d[l], SSM_HEAD_DIM)[None], "ng": ssm_norm_g[l][None],
        "tri3": tri3, "e3": e3,
    }

    yp, conf_p, ssmc_p, st_p = _prompt_call(x_prompt, mod[:bp], p)
    st_p = st_p.reshape(1, bp, SSM_HEADS, SSM_HEAD_DIM, SSM_STATE)

    mod_s = mod[bp:]
    x_pm = x_sample.transpose(1, 0, 2)
    cwsh = jnp.stack([jnp.pad(conf_dw_w[l][:CONV_WIDTH - 1 - t], ((t, 0), (0, 0))) for t in range(nt)])
    swsh = jnp.stack([jnp.pad(ssm_dw_w[l][:SSM_CONV_WIDTH - 1 - t], ((t, 0), (0, 0)))
                      for t in range(SSM_CONV_WIDTH - 1)])
    cs_pm, xs_pm = _sample_conv_state_call(state_conf_conv, state_ssm_conv, l, cwsh, swsh)
    h_pm, bc_pm, unew_pm, xnew_pm, yp_pm, ec_pm, c_pm, b_pm, xd_pm, dec = _sample_pre_call(
        x_pm, mod_s, cs_pm, xs_pm, p)
    st_s, yi_pm = _sample_state_call(dec[:, :SSM_HEADS], state_ssm[l].reshape(nb, SSM_DIM, SSM_STATE),
                                     c_pm, b_pm, xd_pm)

    def flat(v):
        return v.reshape(nt * nb, v.shape[-1])

    ys = _sample_post_call(flat(x_pm), mod_s, flat(h_pm), flat(yp_pm), flat(yi_pm), flat(ec_pm), flat(bc_pm), p)
    ys = ys.reshape(nt, nb, D_MODEL).transpose(1, 0, 2)
    conf_s = _conf_roll_call(state_conf_conv, l, unew_pm.transpose(1, 0, 2))
    ssmc_s = xnew_pm.transpose(1, 0, 2)
    st_s = st_s.reshape(1, nb, SSM_HEADS, SSM_HEAD_DIM, SSM_STATE)
    return (yp, ys, conf_p[None], ssmc_p[None], st_p, conf_s, ssmc_s[None], st_s)
```

```python
import functools

import jax
import jax.numpy as jnp
import numpy as np
from jax import lax
from jax.experimental import pallas as pl
from jax.experimental.pallas import tpu as pltpu

F32 = jnp.float32
BF16 = jnp.bfloat16

D_MODEL = 1024
CONV_WIDTH = 31
SSM_DIM = 2048
SSM_HEADS = 32
SSM_HEAD_DIM = 64
SSM_GROUPS = 8
SSM_STATE = 128
SSM_CONV_WIDTH = 4
SSM_BC = SSM_GROUPS * SSM_STATE
SSM_XBC = SSM_DIM + 2 * SSM_BC
SSM_CHUNK = 128
GROUP_DIM = SSM_DIM // SSM_GROUPS
HEADS_PER_GROUP = SSM_HEADS // SSM_GROUPS
EPS = 1e-6

SUBLANES = 8
LANES = 128
VMEM_LIMIT = 60 * 1024 * 1024

ZM_COLS = 512
ZM_CHUNKS = 2 * SSM_DIM // ZM_COLS
Z_CHUNK0 = 3 * D_MODEL // ZM_COLS
X_CHUNK0 = Z_CHUNK0 + SSM_DIM // ZM_COLS
BIG_CHUNKS = X_CHUNK0 + SSM_XBC // ZM_COLS

PROMPT_TILE = 256
CONF_HALO = 32
SSM_HALO = 8

CONV_STATE_SEQ_BLOCK = 16
SAMPLE_SEQ_BLOCK = 32
STATE_SEQ_BLOCK = 8
POST_ROWS = 256


def _sig(x):
    return 0.5 * jnp.tanh(0.5 * x) + 0.5


def _silu(x):
    return x * _sig(x)


def _w(ref):
    return ref[...]


def _chunk(ref, i):
    return ref[i]


def _cdot(a, w_ref, c0=0, c1=None):
    c1 = w_ref.shape[0] * ZM_COLS if c1 is None else c1
    return jnp.concatenate([_dot(a, w_ref[c]) for c in range(c0 // ZM_COLS, c1 // ZM_COLS)], axis=1)


def _dot(a, b):
    return jnp.dot(a, b, preferred_element_type=F32)


def _dot_nt(a, b):
    return lax.dot_general(a, b, (((1,), (1,)), ((), ())), preferred_element_type=F32)


def _dot_tn(a, b):
    return lax.dot_general(a, b, (((0,), (0,)), ((), ())), preferred_element_type=F32)


def _split3(v):
    hi = v.astype(BF16)
    r1 = v - hi.astype(F32)
    mid = r1.astype(BF16)
    lo = (r1 - mid.astype(F32)).astype(BF16)
    return hi, mid, lo


def _exact_left(sel3, v):
    hi, mid, lo = _split3(v)
    return _dot(sel3, jnp.concatenate([hi, mid, lo], axis=0))


def _expand_heads(v, e3):
    lane = lax.broadcasted_iota(jnp.int32, v.shape, 1)
    v = jnp.where(lane < SSM_HEADS, v, 0.0)
    hi = v.astype(BF16).astype(F32)
    r1 = v - hi
    mid = r1.astype(BF16).astype(F32)
    packed = hi + pltpu.roll(mid, SSM_HEADS, 1) + pltpu.roll(r1 - mid, 2 * SSM_HEADS, 1)
    return _dot(packed.astype(BF16), e3)


def _softplus(x):
    return jnp.maximum(x, 0.0) + jnp.log1p(jnp.exp(-jnp.abs(x)))


def _prenorm(x, g_pre, scale, shift):
    r = lax.rsqrt(jnp.mean(x * x, axis=-1, keepdims=True) + EPS)
    return ((x * r) * g_pre) * (1.0 + scale) + shift


def _layernorm(x, g, b):
    mu = jnp.mean(x, axis=-1, keepdims=True)
    xc = x - mu
    var = jnp.mean(xc * xc, axis=-1, keepdims=True)
    return (xc * lax.rsqrt(var + EPS)) * g + b


def _group_rmsnorm(y, g):
    outs = []
    for gi in range(SSM_GROUPS):
        yg = y[:, gi * GROUP_DIM:(gi + 1) * GROUP_DIM]
        r = lax.rsqrt(jnp.mean(yg * yg, axis=-1, keepdims=True) + EPS)
        outs.append((yg * r) * g[:, gi * GROUP_DIM:(gi + 1) * GROUP_DIM])
    return jnp.concatenate(outs, axis=1)


def _final(x, gate, o, g_post):
    r = lax.rsqrt(jnp.mean(o * o, axis=-1, keepdims=True) + EPS)
    return x + gate * ((o * r) * g_post)


CONV_ROW_STRIDE = 4
CONV_WINDOW = 128


def _to_slabs(buf, v, row0, slab0):
    for s in range(v.shape[1] // LANES):
        buf[slab0 + s, row0:row0 + v.shape[0], :] = v[:, s * LANES:(s + 1) * LANES]


def _conv_slab(buf, w_ref, b_ref, out, s, width, halo, rows, act):
    base = halo - (width - 1)
    n = CONV_WINDOW // CONV_ROW_STRIDE
    for w0 in range(0, rows, CONV_WINDOW):
        accs = [jnp.broadcast_to(b_ref[s], (n, LANES)) for _ in range(CONV_ROW_STRIDE)]
        for d in range(width + CONV_ROW_STRIDE - 1):
            v = buf[s, pl.ds(base + w0 + d, n, stride=CONV_ROW_STRIDE), :]
            for o in range(CONV_ROW_STRIDE):
                if 0 <= d - o < width:
                    accs[o] = accs[o] + jnp.broadcast_to(w_ref[s, d - o:d - o + 1, :], (n, LANES)) * v
        for o in range(CONV_ROW_STRIDE):
            out[s, pl.ds(w0 + o, n, stride=CONV_ROW_STRIDE), :] = act(accs[o])


def _carry_slabs(buf, tail_out, width, halo, rows):
    for s in range(buf.shape[0]):
        tail_out[:, s * LANES:(s + 1) * LANES] = buf[s, rows + halo - (width - 1):rows + halo, :]
        buf[s, 0:halo, :] = buf[s, rows:rows + halo, :]


def _ssd_intra_group(cg, bg, cum, cum_t, mask, xdt_g, g):
    rows = cg.shape[0]
    cb = _dot_nt(cg, bg)
    lane = lax.broadcasted_iota(jnp.int32, (rows, GROUP_DIM), 1)
    ms, xbd = [], []
    for r in range(HEADS_PER_GROUP):
        hd = g * HEADS_PER_GROUP + r
        seg = cum[:, hd:hd + 1] - cum_t[hd:hd + 1, :]
        dec = jnp.exp(jnp.where(mask, seg, -jnp.inf))
        ms.append((cb * dec).astype(BF16))
        in_head = (lane >= r * SSM_HEAD_DIM) & (lane < (r + 1) * SSM_HEAD_DIM)
        xbd.append(jnp.where(in_head, xdt_g, jnp.zeros_like(xdt_g)))
    return _dot(jnp.concatenate(ms, axis=1), jnp.concatenate(xbd, axis=0))


def _cast_chunk_kernel(w_ref, o_ref):
    o_ref[...] = w_ref[...].astype(BF16)


def _cast_chunks(w, layer, n_chunks):
    k = w.shape[1]
    return pl.pallas_call(
        _cast_chunk_kernel,
        grid=(n_chunks,),
        in_specs=[pl.BlockSpec((None, k, ZM_COLS), lambda j: (layer, 0, j))],
        out_specs=pl.BlockSpec((None, k, ZM_COLS), lambda j: (j, 0, 0)),
        out_shape=jax.ShapeDtypeStruct((n_chunks, k, ZM_COLS), BF16),
        name="cast_chunks",
    )(w)


def _ada_kernel(c_ref, w_ref, b_ref, o_ref):
    s = _silu(c_ref[...]).astype(BF16)
    o_ref[...] = _dot(s, w_ref[...].astype(BF16)) + b_ref[...]


def _ada(c_all, w_ada, b_ada):
    n = c_all.shape[0]
    return pl.pallas_call(
        _ada_kernel,
        grid=(3,),
        in_specs=[
            pl.BlockSpec((n, D_MODEL), lambda k: (0, 0)),
            pl.BlockSpec((D_MODEL, D_MODEL), lambda k: (0, k)),
            pl.BlockSpec((1, D_MODEL), lambda k: (0, k)),
        ],
        out_specs=pl.BlockSpec((n, D_MODEL), lambda k: (0, k)),
        out_shape=jax.ShapeDtypeStruct((n, 3 * D_MODEL), F32),
        name="ada_mod",
    )(c_all, w_ada, b_ada)


def _prompt_kernel(x_ref, sh_ref, sc_ref, gt_ref, gpre_ref, gpost_ref, cw_ref, cb_ref, lng_ref, lnb_ref,
                   sw_ref, sb_ref, dtb_ref, alog_ref, dexp_ref, ng_ref, tri_ref, e3_ref,
                   wb_ref, wm_ref, wdt_ref, wco_ref, wso_ref, wo_ref,
                   y_out, conf_out, ssmc_out, st_out,
                   h_ref, ubuf, cacc, xbuf, xact, ht_ref, zm_ref, yn_ref):
    T = PROMPT_TILE
    j = pl.program_id(1)

    @pl.when(j == 0)
    def _():
        ubuf[:, 0:CONF_HALO, :] = jnp.zeros((D_MODEL // LANES, CONF_HALO, LANES), F32)
        xbuf[:, 0:SSM_HALO, :] = jnp.zeros((SSM_XBC // LANES, SSM_HALO, LANES), F32)
        ht_ref[...] = jnp.zeros_like(ht_ref)

    h_ref[...] = _prenorm(x_ref[...], gpre_ref[...], sc_ref[...], sh_ref[...]).astype(BF16)

    u = _cdot(h_ref[...], wb_ref, 0, D_MODEL) * _sig(_cdot(h_ref[...], wb_ref, D_MODEL, 2 * D_MODEL))
    _to_slabs(ubuf, u, CONF_HALO, 0)
    c_slabs = D_MODEL // LANES
    spc = ZM_COLS // LANES
    assert c_slabs == ZM_CHUNKS == SSM_XBC // ZM_COLS

    def project_xbc(i):
        _to_slabs(xbuf, _dot(h_ref[...], wb_ref[X_CHUNK0 + i]), SSM_HALO, i * spc)

    project_xbc(0)
    for i in range(ZM_CHUNKS):
        _conv_slab(ubuf, cw_ref, cb_ref, cacc, i, CONV_WIDTH, CONF_HALO, T, lambda v: v)
        for q in range(spc):
            _conv_slab(xbuf, sw_ref, sb_ref, xact, i * spc + q, SSM_CONV_WIDTH, SSM_HALO, T, _silu)
        if i + 1 < ZM_CHUNKS:
            project_xbc(i + 1)
        if i < ZM_CHUNKS // 2:
            zm_ref[i] = _silu(_dot(h_ref[...], wb_ref[Z_CHUNK0 + i]))
        else:
            zm_ref[i] = _sig(_dot(h_ref[...], wm_ref[i - ZM_CHUNKS // 2]))
    _carry_slabs(ubuf, conf_out, CONV_WIDTH, CONF_HALO, T)
    _carry_slabs(xbuf, ssmc_out, SSM_CONV_WIDTH, SSM_HALO, T)
    acc = jnp.concatenate([cacc[s] for s in range(c_slabs)], axis=1)
    uc = _silu(_layernorm(acc, lng_ref[...], lnb_ref[...])) * _silu(
        _cdot(h_ref[...], wb_ref, 2 * D_MODEL, 3 * D_MODEL))
    branch_conv = _dot(uc.astype(BF16), wco_ref[...])
    xs_slabs = SSM_DIM // LANES
    b_slab0 = xs_slabs
    c_slab0 = xs_slabs + SSM_GROUPS
    slabs_per_group = GROUP_DIM // LANES

    dt = _softplus(_dot(h_ref[...], _w(wdt_ref)) + dtb_ref[...])
    dta = dt * (-jnp.exp(alog_ref[...]))

    row = lax.broadcasted_iota(jnp.int32, (SSM_CHUNK, SSM_CHUNK), 0)
    col = lax.broadcasted_iota(jnp.int32, (SSM_CHUNK, SSM_CHUNK), 1)
    causal = col <= row

    for c in range(T // SSM_CHUNK):
        rs = slice(c * SSM_CHUNK, (c + 1) * SSM_CHUNK)
        cum = _exact_left(_w(tri_ref), dta[rs, :])
        cum_last = cum[SSM_CHUNK - 1:SSM_CHUNK, :]
        cum_t = cum.T
        fac = jnp.concatenate([
            dt[rs, :],
            jnp.exp(cum_last - cum),
            jnp.exp(cum),
            jnp.broadcast_to(jnp.exp(cum_last), (SUBLANES, LANES)),
        ], axis=0)
        facx = _expand_heads(fac, _w(e3_ref))
        dt_x = facx[0:SSM_CHUNK]
        dend_x = facx[SSM_CHUNK:2 * SSM_CHUNK]
        ecum_x = facx[2 * SSM_CHUNK:3 * SSM_CHUNK]
        cdec_x = facx[3 * SSM_CHUNK:3 * SSM_CHUNK + 1]
        xdt = jnp.concatenate([xact[s, rs, :] for s in range(xs_slabs)], axis=1) * dt_x
        xdt_b = xdt.astype(BF16)
        xdte_b = (xdt * dend_x).astype(BF16)
        for g in range(SSM_GROUPS):
            gs = slice(g * GROUP_DIM, (g + 1) * GROUP_DIM)
            ns = slice(g * SSM_STATE, (g + 1) * SSM_STATE)
            bg = xact[b_slab0 + g, rs, :].astype(BF16)
            cg = xact[c_slab0 + g, rs, :].astype(BF16)
            htg = ht_ref[:, gs]
            y_g = _dot(cg, htg.astype(BF16)) * ecum_x[:, gs]
            y_g = y_g + _ssd_intra_group(cg, bg, cum, cum_t, causal, xdt_b[:, gs], g)
            _to_slabs(xbuf, y_g, SSM_HALO + c * SSM_CHUNK, g * slabs_per_group)
            ht_ref[:, gs] = cdec_x[:, gs] * htg + _dot_tn(bg, xdte_b[:, gs])

    for g in range(SSM_GROUPS):
        gs = slice(g * GROUP_DIM, (g + 1) * GROUP_DIM)
        sl = range(g * slabs_per_group, (g + 1) * slabs_per_group)
        y_g = jnp.concatenate([xbuf[s, SSM_HALO:SSM_HALO + T, :] for s in sl], axis=1)
        xs_g = jnp.concatenate([xact[s] for s in sl], axis=1)
        zc = g * GROUP_DIM % ZM_COLS
        zs_g = zm_ref[g * GROUP_DIM // ZM_COLS, :, zc:zc + GROUP_DIM]
        y_g = (y_g + dexp_ref[:, gs] * xs_g) * zs_g
        r = lax.rsqrt(jnp.mean(y_g * y_g, axis=-1, keepdims=True) + EPS)
        yn_ref[:, gs] = ((y_g * r) * ng_ref[:, gs]).astype(BF16)
    branch_ssm = _dot(yn_ref[...], wso_ref[...])

    m0 = SSM_DIM // ZM_COLS
    per = D_MODEL // ZM_COLS
    mg_conv = jnp.concatenate([zm_ref[m0 + q] for q in range(per)], axis=1)
    mg_ssm = jnp.concatenate([zm_ref[m0 + per + q] for q in range(per)], axis=1)
    merged = mg_conv * branch_conv + mg_ssm * branch_ssm
    o = _dot(merged.astype(BF16), wo_ref[...])
    y_out[...] = _final(x_ref[...], gt_ref[...], o, gpost_ref[...])

    @pl.when(j == pl.num_programs(1) - 1)
    def _():
        st_out[...] = ht_ref[...].T


def _const(shape, ngrid):
    nd = len(shape)
    return pl.BlockSpec(shape, lambda *_: (0,) * nd, pipeline_mode=pl.Buffered(1))


def _prompt_call(x, mod, p):
    bsz, seq, _ = x.shape
    T = PROMPT_TILE
    nt = seq // T
    mod3 = mod.reshape(bsz, 1, 3 * D_MODEL)
    consts = (p["g_pre"], p["g_post"], p["cw_s"], p["cb_s"], p["lng"], p["lnb"], p["sw_s"], p["sb_s"], p["dtb"],
              p["alog"], p["dexp"], p["ng"], p["tri3"], p["e3"],
              p["w_big"], p["w_m"], p["w_dt"], p["w_co"], p["w_so"], p["w_o"])
    mod_specs = [pl.BlockSpec((None, 1, D_MODEL), functools.partial(lambda k, b, j: (b, 0, k), k))
                 for k in range(3)]
    in_specs = ([pl.BlockSpec((None, T, D_MODEL), lambda b, j: (b, j, 0))] + mod_specs
                + [_const(c.shape, 2) for c in consts])
    out_shape = (
        jax.ShapeDtypeStruct((bsz, seq, D_MODEL), F32),
        jax.ShapeDtypeStruct((bsz, CONV_WIDTH - 1, D_MODEL), F32),
        jax.ShapeDtypeStruct((bsz, SSM_CONV_WIDTH - 1, SSM_XBC), F32),
        jax.ShapeDtypeStruct((bsz, SSM_DIM, SSM_STATE), F32),
    )
    out_specs = (
        pl.BlockSpec((None, T, D_MODEL), lambda b, j: (b, j, 0)),
        pl.BlockSpec((None, CONV_WIDTH - 1, D_MODEL), lambda b, j: (b, 0, 0)),
        pl.BlockSpec((None, SSM_CONV_WIDTH - 1, SSM_XBC), lambda b, j: (b, 0, 0)),
        pl.BlockSpec((None, SSM_DIM, SSM_STATE), lambda b, j: (b, 0, 0)),
    )
    scratch = [
        pltpu.VMEM((T, D_MODEL), BF16),
        pltpu.VMEM((D_MODEL // LANES, T + CONF_HALO, LANES), F32),
        pltpu.VMEM((D_MODEL // LANES, T, LANES), F32),
        pltpu.VMEM((SSM_XBC // LANES, T + SSM_HALO, LANES), F32),
        pltpu.VMEM((SSM_XBC // LANES, T, LANES), F32),
        pltpu.VMEM((SSM_STATE, SSM_DIM), F32),
        pltpu.VMEM((ZM_CHUNKS, T, ZM_COLS), F32),
        pltpu.VMEM((T, SSM_DIM), BF16),
    ]
    return pl.pallas_call(
        _prompt_kernel,
        grid=(bsz, nt),
        in_specs=in_specs,
        out_specs=out_specs,
        out_shape=out_shape,
        scratch_shapes=scratch,
        compiler_params=pltpu.CompilerParams(
            dimension_semantics=("arbitrary", "arbitrary"), vmem_limit_bytes=VMEM_LIMIT),
        name="prompt_layer",
    )(x, mod3, mod3, mod3, *consts)


def _sample_conv_state_kernel(cst_ref, sst_ref, cwsh_ref, swsh_ref, cs_out, xs_out):
    for b in range(cst_ref.shape[0]):
        st = cst_ref[b]
        for t in range(cs_out.shape[0]):
            cs_out[t, b:b + 1, :] = jnp.sum(st * cwsh_ref[t], axis=0, keepdims=True)
        sx = sst_ref[b]
        for t in range(xs_out.shape[0]):
            xs_out[t, b:b + 1, :] = jnp.sum(sx * swsh_ref[t], axis=0, keepdims=True)


def _sample_conv_state_call(cst, sst, cwsh, swsh):
    nb = cst.shape[0]
    sb = CONV_STATE_SEQ_BLOCK
    nt_c, nt_x = cwsh.shape[0], swsh.shape[0]
    return pl.pallas_call(
        _sample_conv_state_kernel,
        grid=(nb // sb,),
        in_specs=[pl.BlockSpec((sb,) + cst.shape[1:], lambda i: (i, 0, 0)),
                  pl.BlockSpec((sb,) + sst.shape[1:], lambda i: (i, 0, 0)),
                  _const(cwsh.shape, 1), _const(swsh.shape, 1)],
        out_specs=(pl.BlockSpec((nt_c, sb, cst.shape[2]), lambda i: (0, i, 0)),
                   pl.BlockSpec((nt_x, sb, sst.shape[2]), lambda i: (0, i, 0))),
        out_shape=(jax.ShapeDtypeStruct((nt_c, nb, cst.shape[2]), F32),
                   jax.ShapeDtypeStruct((nt_x, nb, sst.shape[2]), F32)),
        name="sample_conv_state",
    )(cst, sst, cwsh, swsh)


def _sample_pre_kernel(x_ref, sh_ref, sc_ref, cst_ref, sst_ref,
                       gpre_ref, cw_ref, cb_ref, lng_ref, lnb_ref, sw_ref, sb_ref, dtb_ref, alog_ref,
                       dexp_ref, e3_ref, wb_ref, wdt_ref, wco_ref,
                       h_out, bc_out, unew_out, xnew_out, yp_out, ec_out, c_out, b_out, xd_out, dec_out):
    nt, sb = x_ref.shape[0], x_ref.shape[1]
    rows = nt * sb

    def tile_rows(v):
        return jnp.concatenate([v] * nt, axis=0)

    x = x_ref[...].reshape(rows, D_MODEL)
    hb = _prenorm(x, gpre_ref[...], tile_rows(sc_ref[...]), tile_rows(sh_ref[...])).astype(BF16)
    h_out[...] = hb.reshape(nt, sb, D_MODEL)

    u = _cdot(hb, wb_ref, 0, D_MODEL) * _sig(_cdot(hb, wb_ref, D_MODEL, 2 * D_MODEL))
    unew_out[...] = u.reshape(nt, sb, D_MODEL)

    accs = []
    for t in range(nt):
        acc = cb_ref[...] + cst_ref[t]
        for k in range(CONV_WIDTH - 1 - t, CONV_WIDTH):
            r = t + k - (CONV_WIDTH - 1)
            acc = acc + cw_ref[k:k + 1, :] * u[r * sb:(r + 1) * sb, :]
        accs.append(acc)
    acc = jnp.concatenate(accs, axis=0)
    uc = _silu(_layernorm(acc, lng_ref[...], lnb_ref[...])) * _silu(_cdot(hb, wb_ref, 2 * D_MODEL, 3 * D_MODEL))
    bc_out[...] = _dot(uc.astype(BF16), wco_ref[...]).reshape(nt, sb, D_MODEL)

    xpre = _cdot(hb, wb_ref, X_CHUNK0 * ZM_COLS, BIG_CHUNKS * ZM_COLS)
    xnew_out[...] = xpre[(nt - (SSM_CONV_WIDTH - 1)) * sb:, :].reshape(SSM_CONV_WIDTH - 1, sb, SSM_XBC)

    xaccs = []
    for t in range(nt):
        xacc = jnp.broadcast_to(sb_ref[...], (sb, SSM_XBC))
        if t < SSM_CONV_WIDTH - 1:
            xacc = xacc + sst_ref[t]
        for k in range(max(SSM_CONV_WIDTH - 1 - t, 0), SSM_CONV_WIDTH):
            r = t + k - (SSM_CONV_WIDTH - 1)
            xacc = xacc + sw_ref[k:k + 1, :] * xpre[r * sb:(r + 1) * sb, :]
        xaccs.append(xacc)
    xbc = _silu(jnp.concatenate(xaccs, axis=0))
    xs = xbc[:, 0:SSM_DIM]
    bmf = xbc[:, SSM_DIM:SSM_DIM + SSM_BC]
    cmf = xbc[:, SSM_DIM + SSM_BC:]
    b_out[...] = bmf.reshape(nt, sb, SSM_BC)
    c_out[...] = cmf.reshape(nt, sb, SSM_BC)
    bm = bmf.astype(BF16)
    cm = cmf.astype(BF16)

    dt = _softplus(_dot(hb, _w(wdt_ref)) + dtb_ref[...])
    dta = dt * (-jnp.exp(alog_ref[...]))
    cums = [dta[0:sb, :]]
    for t in range(1, nt):
        cums.append(cums[-1] + dta[t * sb:(t + 1) * sb, :])
    cum = jnp.concatenate(cums, axis=0)
    cum_last = tile_rows(cums[-1])
    dec_out[...] = jnp.exp(cums[-1])
    cum_t = cum.T

    fac = jnp.concatenate([dt, jnp.exp(cum_last - cum), jnp.exp(cum)], axis=0)
    facx = _expand_heads(fac, _w(e3_ref))
    dt_x = facx[0:rows]
    dend_x = facx[rows:2 * rows]
    ec_out[...] = facx[2 * rows:3 * rows].reshape(nt, sb, SSM_DIM)
    xdt = xs * dt_x
    xdt_b = xdt.astype(BF16)
    xd_out[...] = (xdt * dend_x).reshape(nt, sb, SSM_DIM)

    row = lax.broadcasted_iota(jnp.int32, (rows, rows), 0)
    col = lax.broadcasted_iota(jnp.int32, (rows, rows), 1)
    mask = (col <= row) & ((col % sb) == (row % sb))
    ys = []
    for g in range(SSM_GROUPS):
        gs = slice(g * GROUP_DIM, (g + 1) * GROUP_DIM)
        ns = slice(g * SSM_STATE, (g + 1) * SSM_STATE)
        ys.append(_ssd_intra_group(cm[:, ns], bm[:, ns], cum, cum_t, mask, xdt_b[:, gs], g))
    yp = jnp.concatenate(ys, axis=1) + dexp_ref[...] * xs
    yp_out[...] = yp.reshape(nt, sb, SSM_DIM)


def _sample_pre_call(x_pm, mod, cst_t, sst_t, p):
    nt, nb, _ = x_pm.shape
    sb = SAMPLE_SEQ_BLOCK
    consts = (p["g_pre"], p["cw"], p["cb"], p["lng"], p["lnb"], p["sw"], p["sb"], p["dtb"], p["alog"],
              p["dexp"], p["e3"], p["w_big"], p["w_dt"], p["w_co"])

    def pm(width, n=nt):
        return pl.BlockSpec((n, sb, width), lambda i: (0, i, 0))

    in_specs = ([pm(D_MODEL),
                 pl.BlockSpec((sb, D_MODEL), lambda i: (i, 0)),
                 pl.BlockSpec((sb, D_MODEL), lambda i: (i, 1)),
                 pm(D_MODEL), pm(SSM_XBC, SSM_CONV_WIDTH - 1)]
                + [_const(c.shape, 1) for c in consts])

    def sds(n, width, dtype=F32):
        return jax.ShapeDtypeStruct((n, nb, width), dtype)

    out_shape = (sds(nt, D_MODEL, BF16), sds(nt, D_MODEL), sds(nt, D_MODEL),
                 sds(SSM_CONV_WIDTH - 1, SSM_XBC), sds(nt, SSM_DIM), sds(nt, SSM_DIM),
                 sds(nt, SSM_BC), sds(nt, SSM_BC), sds(nt, SSM_DIM),
                 jax.ShapeDtypeStruct((nb, LANES), F32))
    out_specs = (pm(D_MODEL), pm(D_MODEL), pm(D_MODEL), pm(SSM_XBC, SSM_CONV_WIDTH - 1), pm(SSM_DIM),
                 pm(SSM_DIM), pm(SSM_BC), pm(SSM_BC), pm(SSM_DIM),
                 pl.BlockSpec((sb, LANES), lambda i: (i, 0)))
    return pl.pallas_call(
        _sample_pre_kernel,
        grid=(nb // sb,),
        in_specs=in_specs,
        out_specs=out_specs,
        out_shape=out_shape,
        compiler_params=pltpu.CompilerParams(
            dimension_semantics=("arbitrary",), vmem_limit_bytes=VMEM_LIMIT),
        name="sample_pre",
    )(x_pm, mod, mod, cst_t, sst_t, *consts)


def _sample_state_kernel(dec_ref, st_ref, c_ref, b_ref, xd_ref, st_out, yi_out):
    nt, sb = c_ref.shape[0], c_ref.shape[1]
    rows = nt * sb
    i = pl.program_id(0)
    cmat = c_ref[...].reshape(rows, SSM_BC).astype(BF16)
    bmat = b_ref[...].reshape(rows, SSM_BC)
    xd = xd_ref[...].reshape(rows, SSM_DIM)
    seq_of_row = lax.broadcasted_iota(jnp.int32, (rows, GROUP_DIM), 0) % sb
    seq_of_row_n = lax.broadcasted_iota(jnp.int32, (rows, SSM_STATE), 0) % sb
    accs = []
    for g in range(SSM_GROUPS):
        gs = slice(g * GROUP_DIM, (g + 1) * GROUP_DIM)
        ns = slice(g * SSM_STATE, (g + 1) * SSM_STATE)
        cg = cmat[:, ns]
        bg = bmat[:, ns]
        xd_t = xd[:, gs].T.astype(BF16)
        acc = jnp.zeros((rows, GROUP_DIM), F32)
        for b in range(sb):
            mine = seq_of_row == b
            h0 = st_ref[b, gs, :]
            acc = jnp.where(mine, _dot_nt(cg, h0.astype(BF16)), acc)
            b_mine = jnp.where(seq_of_row_n == b, bg, 0.0).astype(BF16)
            st = _dot(xd_t, b_mine)
            for r in range(HEADS_PER_GROUP):
                hs = slice(r * SSM_HEAD_DIM, (r + 1) * SSM_HEAD_DIM)
                dec = dec_ref[i * sb + b, g * HEADS_PER_GROUP + r]
                st_out[b, g * GROUP_DIM + r * SSM_HEAD_DIM:g * GROUP_DIM + (r + 1) * SSM_HEAD_DIM, :] = (
                    dec * h0[hs, :] + st[hs, :])
        accs.append(acc)
    yi_out[...] = jnp.concatenate(accs, axis=1).reshape(nt, sb, SSM_DIM)


def _sample_state_call(dec, state, c_pm, b_pm, xd_pm):
    nt, nb, _ = c_pm.shape
    sb = STATE_SEQ_BLOCK

    def pm(width):
        return pl.BlockSpec((nt, sb, width), lambda i: (0, i, 0))

    st_spec = pl.BlockSpec((sb, SSM_DIM, SSM_STATE), lambda i: (i, 0, 0))
    return pl.pallas_call(
        _sample_state_kernel,
        grid=(nb // sb,),
        in_specs=[pl.BlockSpec(memory_space=pltpu.SMEM), st_spec, pm(SSM_BC), pm(SSM_BC), pm(SSM_DIM)],
        out_specs=(st_spec, pm(SSM_DIM)),
        out_shape=(jax.ShapeDtypeStruct(state.shape, F32), jax.ShapeDtypeStruct((nt, nb, SSM_DIM), F32)),
        compiler_params=pltpu.CompilerParams(
            dimension_semantics=("arbitrary",), vmem_limit_bytes=VMEM_LIMIT),
        name="sample_state",
    )(dec, state, c_pm, b_pm, xd_pm)


def _sample_post_kernel(x_ref, gt_ref, h_ref, yp_ref, yi_ref, ec_ref, bc_ref, ng_ref, gpost_ref,
                        wb_ref, wm_ref, wso_ref, wo_ref, y_out):
    reps = x_ref.shape[0] // gt_ref.shape[0]
    gate = jnp.concatenate([gt_ref[...]] * reps, axis=0)
    hb = h_ref[...]
    z = _cdot(hb, wb_ref, Z_CHUNK0 * ZM_COLS, X_CHUNK0 * ZM_COLS)
    m = _cdot(hb, wm_ref)
    y = (yp_ref[...] + yi_ref[...] * ec_ref[...]) * _silu(z)
    branch_ssm = _dot(_group_rmsnorm(y, ng_ref[...]).astype(BF16), wso_ref[...])
    merged = _sig(m[:, 0:D_MODEL]) * bc_ref[...] + _sig(m[:, D_MODEL:]) * branch_ssm
    o = _dot(merged.astype(BF16), wo_ref[...])
    y_out[...] = _final(x_ref[...], gate, o, gpost_ref[...])


def _sample_post_call(x2, mod, h2, yp2, yi2, ec2, bc2, p):
    n = x2.shape[0]
    nb = mod.shape[0]
    rb = POST_ROWS
    consts = (p["ng"], p["g_post"], p["w_big"], p["w_m"], p["w_so"], p["w_o"])

    def rowblk(width):
        return pl.BlockSpec((rb, width), lambda i: (i, 0))

    in_specs = ([rowblk(D_MODEL), pl.BlockSpec((nb, D_MODEL), lambda i: (0, 2)), rowblk(D_MODEL),
                 rowblk(SSM_DIM), rowblk(SSM_DIM), rowblk(SSM_DIM), rowblk(D_MODEL)]
                + [_const(c.shape, 1) for c in consts])
    return pl.pallas_call(
        _sample_post_kernel,
        grid=(n // rb,),
        in_specs=in_specs,
        out_specs=rowblk(D_MODEL),
        out_shape=jax.ShapeDtypeStruct((n, D_MODEL), F32),
        compiler_params=pltpu.CompilerParams(
            dimension_semantics=("arbitrary",), vmem_limit_bytes=VMEM_LIMIT),
        name="sample_post",
    )(x2, mod, h2, yp2, yi2, ec2, bc2, *consts)


def _conf_roll_kernel(st_ref, u_ref, o_ref):
    keep = (CONV_WIDTH - 1) - u_ref.shape[1]
    o_ref[:, 0:keep, :] = st_ref[:, u_ref.shape[1]:, :]
    o_ref[:, keep:, :] = u_ref[...]


def _conf_roll_call(state, u_new):
    nb, hist, width = state.shape
    nt = u_new.shape[1]
    sb = STATE_SEQ_BLOCK
    return pl.pallas_call(
        _conf_roll_kernel,
        grid=(nb // sb,),
        in_specs=[pl.BlockSpec((sb, hist, width), lambda i: (i, 0, 0)),
                  pl.BlockSpec((sb, nt, width), lambda i: (i, 0, 0))],
        out_specs=pl.BlockSpec((sb, hist, width), lambda i: (i, 0, 0)),
        out_shape=jax.ShapeDtypeStruct(state.shape, F32),
        name="conf_roll",
    )(state, u_new)


def _selection_constants():
    tri = np.tril(np.ones((SSM_CHUNK, SSM_CHUNK), np.float32))
    tri3 = np.concatenate([tri, tri, tri], axis=1)
    e3 = np.zeros((LANES, SSM_DIM), np.float32)
    for piece in range(3):
        for hd in range(SSM_HEADS):
            e3[piece * SSM_HEADS + hd, hd * SSM_HEAD_DIM:(hd + 1) * SSM_HEAD_DIM] = 1.0
    return jnp.asarray(tri3, BF16), jnp.asarray(e3, BF16)


def _pad_lanes(v, n=LANES):
    return jnp.pad(v, ((0, 0), (0, n - v.shape[1])))


def kernel(x_prompt, x_sample, c_prompt, c_sample, state_conf_conv, state_ssm_conv, state_ssm, w_ada, b_ada, g_pre, g_post, w_in, conf_dw_w, conf_dw_b, conf_ln_g, conf_ln_b, w_conf_out, ssm_dw_w, ssm_dw_b, ssm_dt_bias, ssm_a_log, ssm_d, ssm_norm_g, w_ssm_out, w_o):
    bp = x_prompt.shape[0]
    nb, nt, _ = x_sample.shape
    l = 0
    mod = _ada(jnp.concatenate([c_prompt, c_sample], axis=0), w_ada[l], b_ada[l][None])

    wi = w_in[l]
    o_z = 3 * D_MODEL
    o_x = o_z + SSM_DIM
    o_dt = o_x + SSM_XBC
    o_m = o_dt + SSM_HEADS
    tri3, e3 = _selection_constants()

    def col_chunks(w):
        return w.astype(BF16).reshape(w.shape[0], w.shape[1] // ZM_COLS, ZM_COLS).transpose(1, 0, 2)

    def lane_slabs(w, rows):
        w = jnp.pad(w, ((0, rows - w.shape[0]), (0, 0)))
        return w.reshape(rows, w.shape[1] // LANES, LANES).transpose(1, 0, 2)

    cw = jnp.pad(conf_dw_w[l], ((0, 1), (0, 0)))
    p = {
        "w_big": _cast_chunks(w_in, l, BIG_CHUNKS),
        "w_m": col_chunks(wi[:, o_m:]),
        "w_dt": _pad_lanes(wi[:, o_dt:o_m]).astype(BF16),
        "w_co": w_conf_out[l].astype(BF16), "w_so": w_ssm_out[l].astype(BF16), "w_o": w_o[l].astype(BF16),
        "g_pre": g_pre[l][None], "g_post": g_post[l][None],
        "cw": cw, "cb": conf_dw_b[l][None],
        "cw_s": lane_slabs(cw, cw.shape[0]), "cb_s": lane_slabs(conf_dw_b[l][None], 1),
        "lng": conf_ln_g[l][None], "lnb": conf_ln_b[l][None],
        "sw": ssm_dw_w[l], "sb": ssm_dw_b[l][None],
        "sw_s": lane_slabs(ssm_dw_w[l], SUBLANES), "sb_s": lane_slabs(ssm_dw_b[l][None], 1),
        "dtb": _pad_lanes(ssm_dt_bias[l][None]), "alog": _pad_lanes(ssm_a_log[l][None]),
        "dexp": jnp.repeat(ssm_d[l], SSM_HEAD_DIM)[None], "ng": ssm_norm_g[l][None],
        "tri3": tri3, "e3": e3,
    }

    yp, conf_p, ssmc_p, st_p = _prompt_call(x_prompt, mod[:bp], p)
    st_p = st_p.reshape(1, bp, SSM_HEADS, SSM_HEAD_DIM, SSM_STATE)

    mod_s = mod[bp:]
    x_pm = x_sample.transpose(1, 0, 2)
    cwsh = jnp.stack([jnp.pad(conf_dw_w[l][:CONV_WIDTH - 1 - t], ((t, 0), (0, 0))) for t in range(nt)])
    swsh = jnp.stack([jnp.pad(ssm_dw_w[l][:SSM_CONV_WIDTH - 1 - t], ((t, 0), (0, 0)))
                      for t in range(SSM_CONV_WIDTH - 1)])
    cs_pm, xs_pm = _sample_conv_state_call(state_conf_conv[l], state_ssm_conv[l], cwsh, swsh)
    h_pm, bc_pm, unew_pm, xnew_pm, yp_pm, ec_pm, c_pm, b_pm, xd_pm, dec = _sample_pre_call(
        x_pm, mod_s, cs_pm, xs_pm, p)
    st_s, yi_pm = _sample_state_call(dec[:, :SSM_HEADS], state_ssm[l].reshape(nb, SSM_DIM, SSM_STATE),
                                     c_pm, b_pm, xd_pm)

    def flat(v):
        return v.reshape(nt * nb, v.shape[-1])

    ys = _sample_post_call(flat(x_pm), mod_s, flat(h_pm), flat(yp_pm), flat(yi_pm), flat(ec_pm), flat(bc_pm), p)
    ys = ys.reshape(nt, nb, D_MODEL).transpose(1, 0, 2)
    conf_s = _conf_roll_call(state_conf_conv[l], unew_pm.transpose(1, 0, 2))
    ssmc_s = xnew_pm.transpose(1, 0, 2)
    st_s = st_s.reshape(1, nb, SSM_HEADS, SSM_HEAD_DIM, SSM_STATE)
    return (yp, ys, conf_p[None], ssmc_p[None], st_p, conf_s[None], ssmc_s[None], st_s)
```

```python
import functools

import jax
import jax.numpy as jnp
import numpy as np
from jax import lax
from jax.experimental import pallas as pl
from jax.experimental.pallas import tpu as pltpu

F32 = jnp.float32
BF16 = jnp.bfloat16

D_MODEL = 1024
CONV_WIDTH = 31
SSM_DIM = 2048
SSM_HEADS = 32
SSM_HEAD_DIM = 64
SSM_GROUPS = 8
SSM_STATE = 128
SSM_CONV_WIDTH = 4
SSM_BC = SSM_GROUPS * SSM_STATE
SSM_XBC = SSM_DIM + 2 * SSM_BC
SSM_CHUNK = 128
GROUP_DIM = SSM_DIM // SSM_GROUPS
HEADS_PER_GROUP = SSM_HEADS // SSM_GROUPS
EPS = 1e-6

SUBLANES = 8
LANES = 128
VMEM_LIMIT = 60 * 1024 * 1024

ZM_COLS = 512
ZM_CHUNKS = 2 * SSM_DIM // ZM_COLS

PROMPT_TILE = 256
CONF_HALO = 32
SSM_HALO = 8

SAMPLE_SEQ_BLOCK = 32
STATE_SEQ_BLOCK = 8
POST_ROWS = 256


def _sig(x):
    return 0.5 * jnp.tanh(0.5 * x) + 0.5


def _silu(x):
    return x * _sig(x)


def _dot(a, b):
    return jnp.dot(a, b, preferred_element_type=F32)


def _dot_nt(a, b):
    return lax.dot_general(a, b, (((1,), (1,)), ((), ())), preferred_element_type=F32)


def _dot_tn(a, b):
    return lax.dot_general(a, b, (((0,), (0,)), ((), ())), preferred_element_type=F32)


def _split3(v):
    hi = v.astype(BF16)
    r1 = v - hi.astype(F32)
    mid = r1.astype(BF16)
    lo = (r1 - mid.astype(F32)).astype(BF16)
    return hi, mid, lo


def _exact_left(sel3, v):
    hi, mid, lo = _split3(v)
    return _dot(sel3, jnp.concatenate([hi, mid, lo], axis=0))


def _expand_heads(v, e3):
    return _dot(_pack_heads(v), e3)


def _pack_heads(v):
    lane = lax.broadcasted_iota(jnp.int32, v.shape, 1)
    v = jnp.where(lane < SSM_HEADS, v, 0.0)
    hi = v.astype(BF16).astype(F32)
    r1 = v - hi
    mid = r1.astype(BF16).astype(F32)
    packed = hi + pltpu.roll(mid, SSM_HEADS, 1) + pltpu.roll(r1 - mid, 2 * SSM_HEADS, 1)
    return packed.astype(BF16)


def _softplus(x):
    return jnp.maximum(x, 0.0) + jnp.log1p(jnp.exp(-jnp.abs(x)))


def _prenorm(x, g_pre, scale, shift):
    r = lax.rsqrt(jnp.mean(x * x, axis=-1, keepdims=True) + EPS)
    return ((x * r) * g_pre) * (1.0 + scale) + shift


def _layernorm(x, g, b):
    mu = jnp.mean(x, axis=-1, keepdims=True)
    xc = x - mu
    var = jnp.mean(xc * xc, axis=-1, keepdims=True)
    return (xc * lax.rsqrt(var + EPS)) * g + b


def _group_rmsnorm(y, g):
    outs = []
    for gi in range(SSM_GROUPS):
        yg = y[:, gi * GROUP_DIM:(gi + 1) * GROUP_DIM]
        r = lax.rsqrt(jnp.mean(yg * yg, axis=-1, keepdims=True) + EPS)
        outs.append((yg * r) * g[:, gi * GROUP_DIM:(gi + 1) * GROUP_DIM])
    return jnp.concatenate(outs, axis=1)


def _final(x, gate, o, g_post):
    r = lax.rsqrt(jnp.mean(o * o, axis=-1, keepdims=True) + EPS)
    return x + gate * ((o * r) * g_post)


CONV_ROW_STRIDE = 4
CONV_WINDOW = 128


def _to_slabs(buf, v, row0, slab0):
    for s in range(v.shape[1] // LANES):
        buf[slab0 + s, row0:row0 + v.shape[0], :] = v[:, s * LANES:(s + 1) * LANES]


def _conv_slab(buf, w_ref, b_ref, out, s, width, halo, rows, act):
    base = halo - (width - 1)
    n = CONV_WINDOW // CONV_ROW_STRIDE
    for w0 in range(0, rows, CONV_WINDOW):
        accs = [jnp.broadcast_to(b_ref[s], (n, LANES)) for _ in range(CONV_ROW_STRIDE)]
        for d in range(width + CONV_ROW_STRIDE - 1):
            v = buf[s, pl.ds(base + w0 + d, n, stride=CONV_ROW_STRIDE), :]
            for o in range(CONV_ROW_STRIDE):
                if 0 <= d - o < width:
                    accs[o] = accs[o] + jnp.broadcast_to(w_ref[s, d - o:d - o + 1, :], (n, LANES)) * v
        for o in range(CONV_ROW_STRIDE):
            out[s, pl.ds(w0 + o, n, stride=CONV_ROW_STRIDE), :] = act(accs[o])


def _carry_slabs(buf, tail_out, width, halo, rows):
    for s in range(buf.shape[0]):
        tail_out[:, s * LANES:(s + 1) * LANES] = buf[s, rows + halo - (width - 1):rows + halo, :]
        buf[s, 0:halo, :] = buf[s, rows:rows + halo, :]


def _ssd_intra_group(cg, bg, cum, cum_t, mask, xdt_g, g):
    rows = cg.shape[0]
    cb = _dot_nt(cg, bg)
    lane = lax.broadcasted_iota(jnp.int32, (rows, GROUP_DIM), 1)
    ms, xbd = [], []
    for r in range(HEADS_PER_GROUP):
        hd = g * HEADS_PER_GROUP + r
        seg = cum[:, hd:hd + 1] - cum_t[hd:hd + 1, :]
        dec = jnp.exp(jnp.where(mask, seg, -jnp.inf))
        ms.append((cb * dec).astype(BF16))
        in_head = (lane >= r * SSM_HEAD_DIM) & (lane < (r + 1) * SSM_HEAD_DIM)
        xbd.append(jnp.where(in_head, xdt_g, jnp.zeros_like(xdt_g)))
    return _dot(jnp.concatenate(ms, axis=1), jnp.concatenate(xbd, axis=0))


def _ada_kernel(c_ref, w_ref, b_ref, o_ref):
    s = _silu(c_ref[...]).astype(BF16)
    o_ref[...] = _dot(s, w_ref[...].astype(BF16)) + b_ref[...]


def _ada(c_all, w_ada, b_ada):
    n = c_all.shape[0]
    return pl.pallas_call(
        _ada_kernel,
        grid=(3,),
        in_specs=[
            pl.BlockSpec((n, D_MODEL), lambda k: (0, 0)),
            pl.BlockSpec((D_MODEL, D_MODEL), lambda k: (0, k)),
            pl.BlockSpec((1, D_MODEL), lambda k: (0, k)),
        ],
        out_specs=pl.BlockSpec((n, D_MODEL), lambda k: (0, k)),
        out_shape=jax.ShapeDtypeStruct((n, 3 * D_MODEL), F32),
        name="ada_mod",
    )(c_all, w_ada, b_ada)


def _prompt_kernel(x_ref, sh_ref, sc_ref, gt_ref, gpre_ref, gpost_ref, cw_ref, cb_ref, lng_ref, lnb_ref,
                   sw_ref, sb_ref, dtb_ref, alog_ref, dexp_ref, ng_ref, tri_ref, e3_ref,
                   wv_ref, wx_ref, wzm_ref, wdt_ref, wco_ref, wso_ref, wo_ref,
                   y_out, conf_out, ssmc_out, st_out,
                   h_ref, ubuf, cacc, xbuf, xact, ht_ref, zm_ref, yn_ref):
    T = PROMPT_TILE
    j = pl.program_id(1)

    @pl.when(j == 0)
    def _():
        ubuf[:, 0:CONF_HALO, :] = jnp.zeros((D_MODEL // LANES, CONF_HALO, LANES), F32)
        xbuf[:, 0:SSM_HALO, :] = jnp.zeros((SSM_XBC // LANES, SSM_HALO, LANES), F32)
        ht_ref[...] = jnp.zeros_like(ht_ref)

    h_ref[...] = _prenorm(x_ref[...], gpre_ref[...], sc_ref[...], sh_ref[...]).astype(BF16)

    for c0 in range(0, D_MODEL, ZM_COLS):
        u = _dot(h_ref[...], wv_ref[:, c0:c0 + ZM_COLS]) * _sig(
            _dot(h_ref[...], wv_ref[:, D_MODEL + c0:D_MODEL + c0 + ZM_COLS]))
        _to_slabs(ubuf, u, CONF_HALO, c0 // LANES)
    c_slabs = D_MODEL // LANES
    spc = ZM_COLS // LANES
    assert c_slabs == ZM_CHUNKS == SSM_XBC // ZM_COLS

    def project_xbc(i):
        _to_slabs(xbuf, _dot(h_ref[...], wx_ref[i]), SSM_HALO, i * spc)

    project_xbc(0)
    for i in range(ZM_CHUNKS):
        _conv_slab(ubuf, cw_ref, cb_ref, cacc, i, CONV_WIDTH, CONF_HALO, T, lambda v: v)
        for q in range(spc):
            _conv_slab(xbuf, sw_ref, sb_ref, xact, i * spc + q, SSM_CONV_WIDTH, SSM_HALO, T, _silu)
        if i + 1 < ZM_CHUNKS:
            project_xbc(i + 1)
        zm_ref[i] = (_silu if i < ZM_CHUNKS // 2 else _sig)(_dot(h_ref[...], wzm_ref[i]))
    _carry_slabs(ubuf, conf_out, CONV_WIDTH, CONF_HALO, T)
    _carry_slabs(xbuf, ssmc_out, SSM_CONV_WIDTH, SSM_HALO, T)
    acc = jnp.concatenate([cacc[s] for s in range(c_slabs)], axis=1)
    uc = _silu(_layernorm(acc, lng_ref[...], lnb_ref[...])) * _silu(
        _dot(h_ref[...], wv_ref[:, 2 * D_MODEL:3 * D_MODEL]))
    branch_conv = _dot(uc.astype(BF16), wco_ref[...])
    xs_slabs = SSM_DIM // LANES
    b_slab0 = xs_slabs
    c_slab0 = xs_slabs + SSM_GROUPS
    slabs_per_group = GROUP_DIM // LANES

    dt = _softplus(_dot(h_ref[...], wdt_ref[...]) + dtb_ref[...])
    dta = dt * (-jnp.exp(alog_ref[...]))

    row = lax.broadcasted_iota(jnp.int32, (SSM_CHUNK, SSM_CHUNK), 0)
    col = lax.broadcasted_iota(jnp.int32, (SSM_CHUNK, SSM_CHUNK), 1)
    causal = col <= row

    for c in range(T // SSM_CHUNK):
        rs = slice(c * SSM_CHUNK, (c + 1) * SSM_CHUNK)
        cum = _exact_left(tri_ref[...], dta[rs, :])
        cum_last = cum[SSM_CHUNK - 1:SSM_CHUNK, :]
        cum_t = cum.T
        fac = jnp.concatenate([
            dt[rs, :],
            jnp.exp(cum_last - cum),
            jnp.exp(cum),
            jnp.broadcast_to(jnp.exp(cum_last), (SUBLANES, LANES)),
        ], axis=0)
        fac_b = _pack_heads(fac)
        for g in range(SSM_GROUPS):
            gs = slice(g * GROUP_DIM, (g + 1) * GROUP_DIM)
            facx = _dot(fac_b, e3_ref[:, gs])
            dt_x = facx[0:SSM_CHUNK]
            dend_x = facx[SSM_CHUNK:2 * SSM_CHUNK]
            ecum_x = facx[2 * SSM_CHUNK:3 * SSM_CHUNK]
            cdec_x = facx[3 * SSM_CHUNK:3 * SSM_CHUNK + 1]
            sl = range(g * slabs_per_group, (g + 1) * slabs_per_group)
            xdt = jnp.concatenate([xact[s, rs, :] for s in sl], axis=1) * dt_x
            xdt_b = xdt.astype(BF16)
            xdte_b = (xdt * dend_x).astype(BF16)
            bg = xact[b_slab0 + g, rs, :].astype(BF16)
            cg = xact[c_slab0 + g, rs, :].astype(BF16)
            htg = ht_ref[:, gs]
            y_g = _dot(cg, htg.astype(BF16)) * ecum_x
            y_g = y_g + _ssd_intra_group(cg, bg, cum, cum_t, causal, xdt_b, g)
            _to_slabs(xbuf, y_g, SSM_HALO + c * SSM_CHUNK, g * slabs_per_group)
            ht_ref[:, gs] = cdec_x * htg + _dot_tn(bg, xdte_b)

    for g in range(SSM_GROUPS):
        gs = slice(g * GROUP_DIM, (g + 1) * GROUP_DIM)
        sl = range(g * slabs_per_group, (g + 1) * slabs_per_group)
        y_g = jnp.concatenate([xbuf[s, SSM_HALO:SSM_HALO + T, :] for s in sl], axis=1)
        xs_g = jnp.concatenate([xact[s] for s in sl], axis=1)
        zc = g * GROUP_DIM % ZM_COLS
        zs_g = zm_ref[g * GROUP_DIM // ZM_COLS, :, zc:zc + GROUP_DIM]
        y_g = (y_g + dexp_ref[:, gs] * xs_g) * zs_g
        r = lax.rsqrt(jnp.mean(y_g * y_g, axis=-1, keepdims=True) + EPS)
        yn_ref[:, gs] = ((y_g * r) * ng_ref[:, gs]).astype(BF16)
    branch_ssm = _dot(yn_ref[...], wso_ref[...])

    m0 = SSM_DIM // ZM_COLS
    per = D_MODEL // ZM_COLS
    mg_conv = jnp.concatenate([zm_ref[m0 + q] for q in range(per)], axis=1)
    mg_ssm = jnp.concatenate([zm_ref[m0 + per + q] for q in range(per)], axis=1)
    merged = mg_conv * branch_conv + mg_ssm * branch_ssm
    o = _dot(merged.astype(BF16), wo_ref[...])
    y_out[...] = _final(x_ref[...], gt_ref[...], o, gpost_ref[...])

    @pl.when(j == pl.num_programs(1) - 1)
    def _():
        st_out[...] = ht_ref[...].T


def _const(shape, ngrid):
    nd = len(shape)
    return pl.BlockSpec(shape, lambda *_: (0,) * nd, pipeline_mode=pl.Buffered(1))


def _prompt_call(x, mod, p):
    bsz, seq, _ = x.shape
    T = PROMPT_TILE
    nt = seq // T
    mod3 = mod.reshape(bsz, 1, 3 * D_MODEL)
    consts = (p["g_pre"], p["g_post"], p["cw_s"], p["cb_s"], p["lng"], p["lnb"], p["sw_s"], p["sb_s"], p["dtb"],
              p["alog"], p["dexp"], p["ng"], p["tri3"], p["e3"],
              p["w_v"], p["w_x"], p["w_zm"], p["w_dt"], p["w_co"], p["w_so"], p["w_o"])
    mod_specs = [pl.BlockSpec((None, 1, D_MODEL), functools.partial(lambda k, b, j: (b, 0, k), k))
                 for k in range(3)]
    in_specs = ([pl.BlockSpec((None, T, D_MODEL), lambda b, j: (b, j, 0))] + mod_specs
                + [_const(c.shape, 2) for c in consts])
    out_shape = (
        jax.ShapeDtypeStruct((bsz, seq, D_MODEL), F32),
        jax.ShapeDtypeStruct((bsz, CONV_WIDTH - 1, D_MODEL), F32),
        jax.ShapeDtypeStruct((bsz, SSM_CONV_WIDTH - 1, SSM_XBC), F32),
        jax.ShapeDtypeStruct((bsz, SSM_DIM, SSM_STATE), F32),
    )
    out_specs = (
        pl.BlockSpec((None, T, D_MODEL), lambda b, j: (b, j, 0)),
        pl.BlockSpec((None, CONV_WIDTH - 1, D_MODEL), lambda b, j: (b, 0, 0)),
        pl.BlockSpec((None, SSM_CONV_WIDTH - 1, SSM_XBC), lambda b, j: (b, 0, 0)),
        pl.BlockSpec((None, SSM_DIM, SSM_STATE), lambda b, j: (b, 0, 0)),
    )
    scratch = [
        pltpu.VMEM((T, D_MODEL), BF16),
        pltpu.VMEM((D_MODEL // LANES, T + CONF_HALO, LANES), F32),
        pltpu.VMEM((D_MODEL // LANES, T, LANES), F32),
        pltpu.VMEM((SSM_XBC // LANES, T + SSM_HALO, LANES), F32),
        pltpu.VMEM((SSM_XBC // LANES, T, LANES), F32),
        pltpu.VMEM((SSM_STATE, SSM_DIM), F32),
        pltpu.VMEM((ZM_CHUNKS, T, ZM_COLS), F32),
        pltpu.VMEM((T, SSM_DIM), BF16),
    ]
    return pl.pallas_call(
        _prompt_kernel,
        grid=(bsz, nt),
        in_specs=in_specs,
        out_specs=out_specs,
        out_shape=out_shape,
        scratch_shapes=scratch,
        compiler_params=pltpu.CompilerParams(
            dimension_semantics=("arbitrary", "arbitrary"), vmem_limit_bytes=VMEM_LIMIT),
        name="prompt_layer",
    )(x, mod3, mod3, mod3, *consts)


def _sample_pre_kernel(x_ref, sh_ref, sc_ref, cst_ref, sst_ref,
                       gpre_ref, cw_ref, cb_ref, lng_ref, lnb_ref, sw_ref, sb_ref, dtb_ref, alog_ref,
                       dexp_ref, e3_ref, wv_ref, wx_ref, wdt_ref, wco_ref,
                       h_out, bc_out, unew_out, xnew_out, yp_out, ec_out, c_out, b_out, xd_out, dec_out):
    nt, sb = x_ref.shape[0], x_ref.shape[1]
    rows = nt * sb

    def tile_rows(v):
        return jnp.concatenate([v] * nt, axis=0)

    x = x_ref[...].reshape(rows, D_MODEL)
    hb = _prenorm(x, gpre_ref[...], tile_rows(sc_ref[...]), tile_rows(sh_ref[...])).astype(BF16)
    h_out[...] = hb.reshape(nt, sb, D_MODEL)

    u = _dot(hb, wv_ref[:, 0:D_MODEL]) * _sig(_dot(hb, wv_ref[:, D_MODEL:2 * D_MODEL]))
    unew_out[...] = u.reshape(nt, sb, D_MODEL)

    def conf_ext(jj):
        if jj < CONV_WIDTH - 1:
            return cst_ref[jj]
        return u[(jj - (CONV_WIDTH - 1)) * sb:(jj - (CONV_WIDTH - 2)) * sb, :]

    accs = []
    for t in range(nt):
        acc = jnp.broadcast_to(cb_ref[...], (sb, D_MODEL))
        for k in range(CONV_WIDTH):
            acc = acc + cw_ref[k:k + 1, :] * conf_ext(t + k)
        accs.append(acc)
    acc = jnp.concatenate(accs, axis=0)
    uc = _silu(_layernorm(acc, lng_ref[...], lnb_ref[...])) * _silu(_dot(hb, wv_ref[:, 2 * D_MODEL:3 * D_MODEL]))
    bc_out[...] = _dot(uc.astype(BF16), wco_ref[...]).reshape(nt, sb, D_MODEL)

    xpre = jnp.concatenate([_dot(hb, wx_ref[c]) for c in range(wx_ref.shape[0])], axis=1)
    xnew_out[...] = xpre[(nt - (SSM_CONV_WIDTH - 1)) * sb:, :].reshape(SSM_CONV_WIDTH - 1, sb, SSM_XBC)

    def ssm_ext(jj):
        if jj < SSM_CONV_WIDTH - 1:
            return sst_ref[jj]
        return xpre[(jj - (SSM_CONV_WIDTH - 1)) * sb:(jj - (SSM_CONV_WIDTH - 2)) * sb, :]

    xaccs = []
    for t in range(nt):
        xacc = jnp.broadcast_to(sb_ref[...], (sb, SSM_XBC))
        for k in range(SSM_CONV_WIDTH):
            xacc = xacc + sw_ref[k:k + 1, :] * ssm_ext(t + k)
        xaccs.append(xacc)
    xbc = _silu(jnp.concatenate(xaccs, axis=0))
    xs = xbc[:, 0:SSM_DIM]
    bmf = xbc[:, SSM_DIM:SSM_DIM + SSM_BC]
    cmf = xbc[:, SSM_DIM + SSM_BC:]
    b_out[...] = bmf.reshape(nt, sb, SSM_BC)
    c_out[...] = cmf.reshape(nt, sb, SSM_BC)
    bm = bmf.astype(BF16)
    cm = cmf.astype(BF16)

    dt = _softplus(_dot(hb, wdt_ref[...]) + dtb_ref[...])
    dta = dt * (-jnp.exp(alog_ref[...]))
    cums = [dta[0:sb, :]]
    for t in range(1, nt):
        cums.append(cums[-1] + dta[t * sb:(t + 1) * sb, :])
    cum = jnp.concatenate(cums, axis=0)
    cum_last = tile_rows(cums[-1])
    dec_out[...] = jnp.exp(cums[-1])
    cum_t = cum.T

    fac = jnp.concatenate([dt, jnp.exp(cum_last - cum), jnp.exp(cum)], axis=0)
    facx = _expand_heads(fac, e3_ref[...])
    dt_x = facx[0:rows]
    dend_x = facx[rows:2 * rows]
    ec_out[...] = facx[2 * rows:3 * rows].reshape(nt, sb, SSM_DIM)
    xdt = xs * dt_x
    xdt_b = xdt.astype(BF16)
    xd_out[...] = (xdt * dend_x).reshape(nt, sb, SSM_DIM)

    row = lax.broadcasted_iota(jnp.int32, (rows, rows), 0)
    col = lax.broadcasted_iota(jnp.int32, (rows, rows), 1)
    mask = (col <= row) & ((col % sb) == (row % sb))
    ys = []
    for g in range(SSM_GROUPS):
        gs = slice(g * GROUP_DIM, (g + 1) * GROUP_DIM)
        ns = slice(g * SSM_STATE, (g + 1) * SSM_STATE)
        ys.append(_ssd_intra_group(cm[:, ns], bm[:, ns], cum, cum_t, mask, xdt_b[:, gs], g))
    yp = jnp.concatenate(ys, axis=1) + dexp_ref[...] * xs
    yp_out[...] = yp.reshape(nt, sb, SSM_DIM)


def _sample_pre_call(x_pm, mod, cst_t, sst_t, p):
    nt, nb, _ = x_pm.shape
    sb = SAMPLE_SEQ_BLOCK
    consts = (p["g_pre"], p["cw"], p["cb"], p["lng"], p["lnb"], p["sw"], p["sb"], p["dtb"], p["alog"],
              p["dexp"], p["e3"], p["w_v"], p["w_x"], p["w_dt"], p["w_co"])

    def pm(width, n=nt):
        return pl.BlockSpec((n, sb, width), lambda i: (0, i, 0))

    in_specs = ([pm(D_MODEL),
                 pl.BlockSpec((sb, D_MODEL), lambda i: (i, 0)),
                 pl.BlockSpec((sb, D_MODEL), lambda i: (i, 1)),
                 pm(D_MODEL, CONV_WIDTH - 1), pm(SSM_XBC, SSM_CONV_WIDTH - 1)]
                + [_const(c.shape, 1) for c in consts])

    def sds(n, width, dtype=F32):
        return jax.ShapeDtypeStruct((n, nb, width), dtype)

    out_shape = (sds(nt, D_MODEL, BF16), sds(nt, D_MODEL), sds(nt, D_MODEL),
                 sds(SSM_CONV_WIDTH - 1, SSM_XBC), sds(nt, SSM_DIM), sds(nt, SSM_DIM),
                 sds(nt, SSM_BC), sds(nt, SSM_BC), sds(nt, SSM_DIM),
                 jax.ShapeDtypeStruct((nb, LANES), F32))
    out_specs = (pm(D_MODEL), pm(D_MODEL), pm(D_MODEL), pm(SSM_XBC, SSM_CONV_WIDTH - 1), pm(SSM_DIM),
                 pm(SSM_DIM), pm(SSM_BC), pm(SSM_BC), pm(SSM_DIM),
                 pl.BlockSpec((sb, LANES), lambda i: (i, 0)))
    return pl.pallas_call(
        _sample_pre_kernel,
        grid=(nb // sb,),
        in_specs=in_specs,
        out_specs=out_specs,
        out_shape=out_shape,
        compiler_params=pltpu.CompilerParams(
            dimension_semantics=("arbitrary",), vmem_limit_bytes=VMEM_LIMIT),
        name="sample_pre",
    )(x_pm, mod, mod, cst_t, sst_t, *consts)


def _sample_state_kernel(dec_ref, st_ref, c_ref, b_ref, xd_ref, st_out, yi_out):
    nt, sb = c_ref.shape[0], c_ref.shape[1]
    rows = nt * sb
    i = pl.program_id(0)
    cmat = c_ref[...].reshape(rows, SSM_BC).astype(BF16)
    bmat = b_ref[...].reshape(rows, SSM_BC)
    xd = xd_ref[...].reshape(rows, SSM_DIM)
    seq_of_row = lax.broadcasted_iota(jnp.int32, (rows, GROUP_DIM), 0) % sb
    seq_of_row_n = lax.broadcasted_iota(jnp.int32, (rows, SSM_STATE), 0) % sb
    accs = []
    for g in range(SSM_GROUPS):
        gs = slice(g * GROUP_DIM, (g + 1) * GROUP_DIM)
        ns = slice(g * SSM_STATE, (g + 1) * SSM_STATE)
        cg = cmat[:, ns]
        bg = bmat[:, ns]
        xd_t = xd[:, gs].T.astype(BF16)
        acc = jnp.zeros((rows, GROUP_DIM), F32)
        for b in range(sb):
            mine = seq_of_row == b
            h0 = st_ref[b, gs, :]
            acc = jnp.where(mine, _dot_nt(cg, h0.astype(BF16)), acc)
            b_mine = jnp.where(seq_of_row_n == b, bg, 0.0).astype(BF16)
            st = _dot(xd_t, b_mine)
            for r in range(HEADS_PER_GROUP):
                hs = slice(r * SSM_HEAD_DIM, (r + 1) * SSM_HEAD_DIM)
                dec = dec_ref[i * sb + b, g * HEADS_PER_GROUP + r]
                st_out[b, g * GROUP_DIM + r * SSM_HEAD_DIM:g * GROUP_DIM + (r + 1) * SSM_HEAD_DIM, :] = (
                    dec * h0[hs, :] + st[hs, :])
        accs.append(acc)
    yi_out[...] = jnp.concatenate(accs, axis=1).reshape(nt, sb, SSM_DIM)


def _sample_state_call(dec, state, c_pm, b_pm, xd_pm):
    nt, nb, _ = c_pm.shape
    sb = STATE_SEQ_BLOCK

    def pm(width):
        return pl.BlockSpec((nt, sb, width), lambda i: (0, i, 0))

    st_spec = pl.BlockSpec((sb, SSM_DIM, SSM_STATE), lambda i: (i, 0, 0))
    return pl.pallas_call(
        _sample_state_kernel,
        grid=(nb // sb,),
        in_specs=[pl.BlockSpec(memory_space=pltpu.SMEM), st_spec, pm(SSM_BC), pm(SSM_BC), pm(SSM_DIM)],
        out_specs=(st_spec, pm(SSM_DIM)),
        out_shape=(jax.ShapeDtypeStruct(state.shape, F32), jax.ShapeDtypeStruct((nt, nb, SSM_DIM), F32)),
        compiler_params=pltpu.CompilerParams(
            dimension_semantics=("arbitrary",), vmem_limit_bytes=VMEM_LIMIT),
        name="sample_state",
    )(dec, state, c_pm, b_pm, xd_pm)


def _sample_post_kernel(x_ref, gt_ref, h_ref, yp_ref, yi_ref, ec_ref, bc_ref, ng_ref, gpost_ref,
                        wzm_ref, wso_ref, wo_ref, y_out):
    reps = x_ref.shape[0] // gt_ref.shape[0]
    gate = jnp.concatenate([gt_ref[...]] * reps, axis=0)
    hb = h_ref[...]
    zm = [_dot(hb, wzm_ref[c]) for c in range(ZM_CHUNKS)]
    z = jnp.concatenate(zm[:ZM_CHUNKS // 2], axis=1)
    m = jnp.concatenate(zm[ZM_CHUNKS // 2:], axis=1)
    y = (yp_ref[...] + yi_ref[...] * ec_ref[...]) * _silu(z)
    branch_ssm = _dot(_group_rmsnorm(y, ng_ref[...]).astype(BF16), wso_ref[...])
    merged = _sig(m[:, 0:D_MODEL]) * bc_ref[...] + _sig(m[:, D_MODEL:]) * branch_ssm
    o = _dot(merged.astype(BF16), wo_ref[...])
    y_out[...] = _final(x_ref[...], gate, o, gpost_ref[...])


def _sample_post_call(x2, mod, h2, yp2, yi2, ec2, bc2, p):
    n = x2.shape[0]
    nb = mod.shape[0]
    rb = POST_ROWS
    consts = (p["ng"], p["g_post"], p["w_zm"], p["w_so"], p["w_o"])

    def rowblk(width):
        return pl.BlockSpec((rb, width), lambda i: (i, 0))

    in_specs = ([rowblk(D_MODEL), pl.BlockSpec((nb, D_MODEL), lambda i: (0, 2)), rowblk(D_MODEL),
                 rowblk(SSM_DIM), rowblk(SSM_DIM), rowblk(SSM_DIM), rowblk(D_MODEL)]
                + [_const(c.shape, 1) for c in consts])
    return pl.pallas_call(
        _sample_post_kernel,
        grid=(n // rb,),
        in_specs=in_specs,
        out_specs=rowblk(D_MODEL),
        out_shape=jax.ShapeDtypeStruct((n, D_MODEL), F32),
        compiler_params=pltpu.CompilerParams(
            dimension_semantics=("arbitrary",), vmem_limit_bytes=VMEM_LIMIT),
        name="sample_post",
    )(x2, mod, h2, yp2, yi2, ec2, bc2, *consts)


def _conf_roll_kernel(st_ref, u_ref, o_ref):
    keep = (CONV_WIDTH - 1) - u_ref.shape[1]
    o_ref[:, 0:keep, :] = st_ref[:, u_ref.shape[1]:, :]
    o_ref[:, keep:, :] = u_ref[...]


def _conf_roll_call(state, u_new):
    nb, hist, width = state.shape
    nt = u_new.shape[1]
    sb = STATE_SEQ_BLOCK
    return pl.pallas_call(
        _conf_roll_kernel,
        grid=(nb // sb,),
        in_specs=[pl.BlockSpec((sb, hist, width), lambda i: (i, 0, 0)),
                  pl.BlockSpec((sb, nt, width), lambda i: (i, 0, 0))],
        out_specs=pl.BlockSpec((sb, hist, width), lambda i: (i, 0, 0)),
        out_shape=jax.ShapeDtypeStruct(state.shape, F32),
        name="conf_roll",
    )(state, u_new)


def _selection_constants():
    tri = np.tril(np.ones((SSM_CHUNK, SSM_CHUNK), np.float32))
    tri3 = np.concatenate([tri, tri, tri], axis=1)
    e3 = np.zeros((LANES, SSM_DIM), np.float32)
    for piece in range(3):
        for hd in range(SSM_HEADS):
            e3[piece * SSM_HEADS + hd, hd * SSM_HEAD_DIM:(hd + 1) * SSM_HEAD_DIM] = 1.0
    return jnp.asarray(tri3, BF16), jnp.asarray(e3, BF16)


def _pad_lanes(v, n=LANES):
    return jnp.pad(v, ((0, 0), (0, n - v.shape[1])))


def kernel(x_prompt, x_sample, c_prompt, c_sample, state_conf_conv, state_ssm_conv, state_ssm, w_ada, b_ada, g_pre, g_post, w_in, conf_dw_w, conf_dw_b, conf_ln_g, conf_ln_b, w_conf_out, ssm_dw_w, ssm_dw_b, ssm_dt_bias, ssm_a_log, ssm_d, ssm_norm_g, w_ssm_out, w_o):
    bp = x_prompt.shape[0]
    nb, nt, _ = x_sample.shape
    l = 0
    mod = _ada(jnp.concatenate([c_prompt, c_sample], axis=0), w_ada[l], b_ada[l][None])

    wi = w_in[l]
    o_z = 3 * D_MODEL
    o_x = o_z + SSM_DIM
    o_dt = o_x + SSM_XBC
    o_m = o_dt + SSM_HEADS
    tri3, e3 = _selection_constants()
    def col_chunks(w):
        return w.reshape(w.shape[0], w.shape[1] // ZM_COLS, ZM_COLS).transpose(1, 0, 2)

    def lane_slabs(w, rows):
        w = jnp.pad(w, ((0, rows - w.shape[0]), (0, 0)))
        return w.reshape(rows, w.shape[1] // LANES, LANES).transpose(1, 0, 2)

    cw = jnp.pad(conf_dw_w[l], ((0, 1), (0, 0)))
    p = {
        "w_v": wi[:, :o_z].astype(BF16),
        "w_x": col_chunks(wi[:, o_x:o_dt].astype(BF16)),
        "w_zm": col_chunks(jnp.concatenate([wi[:, o_z:o_x], wi[:, o_m:]], axis=1).astype(BF16)),
        "w_dt": _pad_lanes(wi[:, o_dt:o_m]).astype(BF16),
        "w_co": w_conf_out[l].astype(BF16), "w_so": w_ssm_out[l].astype(BF16), "w_o": w_o[l].astype(BF16),
        "g_pre": g_pre[l][None], "g_post": g_post[l][None],
        "cw": cw, "cb": conf_dw_b[l][None],
        "cw_s": lane_slabs(cw, cw.shape[0]), "cb_s": lane_slabs(conf_dw_b[l][None], 1),
        "lng": conf_ln_g[l][None], "lnb": conf_ln_b[l][None],
        "sw": ssm_dw_w[l], "sb": ssm_dw_b[l][None],
        "sw_s": lane_slabs(ssm_dw_w[l], SUBLANES), "sb_s": lane_slabs(ssm_dw_b[l][None], 1),
        "dtb": _pad_lanes(ssm_dt_bias[l][None]), "alog": _pad_lanes(ssm_a_log[l][None]),
        "dexp": jnp.repeat(ssm_d[l], SSM_HEAD_DIM)[None], "ng": ssm_norm_g[l][None],
        "tri3": tri3, "e3": e3,
    }

    yp, conf_p, ssmc_p, st_p = _prompt_call(x_prompt, mod[:bp], p)
    st_p = st_p.reshape(1, bp, SSM_HEADS, SSM_HEAD_DIM, SSM_STATE)

    mod_s = mod[bp:]
    x_pm = x_sample.transpose(1, 0, 2)
    cst_t = state_conf_conv[l].transpose(1, 0, 2)
    sst_t = state_ssm_conv[l].transpose(1, 0, 2)
    h_pm, bc_pm, unew_pm, xnew_pm, yp_pm, ec_pm, c_pm, b_pm, xd_pm, dec = _sample_pre_call(
        x_pm, mod_s, cst_t, sst_t, p)
    st_s, yi_pm = _sample_state_call(dec[:, :SSM_HEADS], state_ssm[l].reshape(nb, SSM_DIM, SSM_STATE),
                                     c_pm, b_pm, xd_pm)

    def flat(v):
        return v.reshape(nt * nb, v.shape[-1])

    ys = _sample_post_call(flat(x_pm), mod_s, flat(h_pm), flat(yp_pm), flat(yi_pm), flat(ec_pm), flat(bc_pm), p)
    ys = ys.reshape(nt, nb, D_MODEL).transpose(1, 0, 2)
    conf_s = _conf_roll_call(state_conf_conv[l], unew_pm.transpose(1, 0, 2))
    ssmc_s = xnew_pm.transpose(1, 0, 2)
    st_s = st_s.reshape(1, nb, SSM_HEADS, SSM_HEAD_DIM, SSM_STATE)
    return (yp, ys, conf_p[None], ssmc_p[None], st_p, conf_s[None], ssmc_s[None], st_s)
```
